```python
import jax, jax.numpy as jnp
from jax import lax
import numpy as np

D_MODEL = 1024
BATCH = 1
SEQ = 16384
DEPTH = 1
DEC_BATCH = 16
DEC_SEQ = 2048
PAST_LEN = 128

GRID_W = 64
N_MEM = 256
F_GROUPS = 4
F_CH = 64
F_WIDTH = F_GROUPS * F_CH
HEAD_DIM = 128
N_Q_HEADS = 6
N_KV_HEADS = 2
Q_PER_KV = N_Q_HEADS // N_KV_HEADS
Q_WIDTH = N_Q_HEADS * HEAD_DIM
KV_WIDTH = N_KV_HEADS * HEAD_DIM
MIX_WIDTH = F_WIDTH + Q_WIDTH
IN_WIDTH = F_WIDTH + Q_WIDTH + 2 * KV_WIDTH
Q_BLOCK = 128
ROPE_AXIS_DIM = HEAD_DIM // 2
ROPE_THETA = 10000.0
X_HEADS = 4
X_HEAD_DIM = D_MODEL // X_HEADS
D_FF = 4 * D_MODEL
EPS = 1e-6

kernel_name = "hybrid_fourier_gqa_axial_encoder"


def rms_norm(x, g):
    xf = x.astype(jnp.float32)
    y = xf * lax.rsqrt(jnp.mean(xf * xf, axis=-1, keepdims=True) + EPS)
    return (y * g.astype(jnp.float32)).astype(x.dtype)


def axial_angles(n_tok):
    rows = n_tok // GRID_W
    r = jnp.repeat(jnp.arange(rows, dtype=jnp.float32), GRID_W)
    c = jnp.tile(jnp.arange(GRID_W, dtype=jnp.float32), rows)
    inv_freq = 1.0 / (ROPE_THETA ** (jnp.arange(0, ROPE_AXIS_DIM, 2, dtype=jnp.float32) / ROPE_AXIS_DIM))
    ang_r = r[:, None] * inv_freq[None, :]
    ang_c = c[:, None] * inv_freq[None, :]
    return (jnp.cos(ang_r)[:, None, :], jnp.sin(ang_r)[:, None, :],
            jnp.cos(ang_c)[:, None, :], jnp.sin(ang_c)[:, None, :])


def rotate(xp, cos, sin):
    half = xp.shape[-1] // 2
    a, b = xp[..., :half], xp[..., half:]
    return jnp.concatenate([a * cos - b * sin, b * cos + a * sin], axis=-1)


def axial_rope(x, angles):
    cr, sr, cc, sc = angles
    xf = x.astype(jnp.float32)
    out = jnp.concatenate([rotate(xf[..., :ROPE_AXIS_DIM], cr, sr),
                           rotate(xf[..., ROPE_AXIS_DIM:], cc, sc)], axis=-1)
    return out.astype(x.dtype)


def fourier_mixer(zf, w_f):
    b, s, _ = zf.shape
    zg = zf.reshape(b, s, F_GROUPS, F_CH).astype(jnp.float32)
    fr = jnp.fft.fft2(zg, axes=(1, 3), norm="ortho").real.astype(zf.dtype)
    out = jnp.einsum('bsgc,gcd->bsgd', fr, w_f)
    return out.reshape(b, s, F_WIDTH)


def block_gqa(q, k, v):
    b, s, _, d = q.shape
    nb = s // Q_BLOCK
    scale = 1.0 / np.sqrt(d).astype(np.float32)
    qb = q.reshape(b, nb, Q_BLOCK, N_KV_HEADS, Q_PER_KV, d).transpose(1, 0, 2, 3, 4, 5)
    kf = k.astype(jnp.float32)

    def one_block(qblk):
        sc = jnp.einsum('bqkgd,bskd->bkgqs', qblk.astype(jnp.float32), kf) * scale
        p = jax.nn.softmax(sc, axis=-1).astype(v.dtype)
        return jnp.einsum('bkgqs,bskd->bqkgd', p, v)

    ob = lax.map(one_block, qb)
    return ob.transpose(1, 0, 2, 3, 4, 5).reshape(b, s, N_Q_HEADS * d)


def memory_cross_attention(h, mem_n, w_cq, w_ckv, g_cq, g_ck, w_co):
    b, s, _ = h.shape
    m = mem_n.shape[1]
    q = rms_norm((h @ w_cq).reshape(b, s, X_HEADS, X_HEAD_DIM), g_cq)
    kv = mem_n @ w_ckv
    k = rms_norm(kv[..., :D_MODEL].reshape(b, m, X_HEADS, X_HEAD_DIM), g_ck)
    v = kv[..., D_MODEL:].reshape(b, m, X_HEADS, X_HEAD_DIM)
    scale = 1.0 / np.sqrt(X_HEAD_DIM).astype(np.float32)
    sc = jnp.einsum('bqhd,bmhd->bhqm', q.astype(jnp.float32), k.astype(jnp.float32)) * scale
    p = jax.nn.softmax(sc, axis=-1).astype(v.dtype)
    o = jnp.einsum('bhqm,bmhd->bqhd', p, v).reshape(b, s, D_MODEL)
    return o @ w_co


def encoder_layer(x, mem, angles, g_mix, w_in, w_fourier, g_q, g_k, w_out,
                  g_cross, g_mem, w_cq, w_ckv, g_cq, g_ck, w_co, g_mlp, w_up, w_down):
    b, s, _ = x.shape
    h = rms_norm(x, g_mix)
    z = h @ w_in
    zf = z[..., :F_WIDTH]
    zq = z[..., F_WIDTH:F_WIDTH + Q_WIDTH]
    zk = z[..., F_WIDTH + Q_WIDTH:F_WIDTH + Q_WIDTH + KV_WIDTH]
    zv = z[..., F_WIDTH + Q_WIDTH + KV_WIDTH:]
    f_out = fourier_mixer(zf, w_fourier)
    q = axial_rope(rms_norm(zq.reshape(b, s, N_Q_HEADS, HEAD_DIM), g_q), angles)
    k = axial_rope(rms_norm(zk.reshape(b, s, N_KV_HEADS, HEAD_DIM), g_k), angles)
    v = zv.reshape(b, s, N_KV_HEADS, HEAD_DIM)
    a_out = block_gqa(q, k, v)
    x = x + jnp.concatenate([f_out, a_out], axis=-1) @ w_out
    x = x + memory_cross_attention(rms_norm(x, g_cross), rms_norm(mem, g_mem),
                                   w_cq, w_ckv, g_cq, g_ck, w_co)
    u = jax.nn.relu(rms_norm(x, g_mlp) @ w_up)
    x = x + (u * u) @ w_down
    return x


def setup_inputs(seed: int = 0) -> dict:
    key = jax.random.key(seed)
    ks = jax.random.split(key, 24)
    f32 = jnp.float32

    def w(k, shape, fan_in, mult=1.0):
        return jax.random.normal(k, shape, f32) * (mult * fan_in ** -0.5)

    def gain(k, shape):
        return 1.0 + 0.02 * jax.random.normal(k, shape, f32)

    L = DEPTH
    return {
        "x_prompt": jax.random.normal(ks[0], (BATCH, SEQ, D_MODEL), f32),
        "x_sample": jax.random.normal(ks[1], (DEC_BATCH, DEC_SEQ, D_MODEL), f32),
        "mem_prompt": jax.random.normal(ks[2], (BATCH, N_MEM, D_MODEL), f32),
        "mem_sample": jax.random.normal(ks[3], (DEC_BATCH, N_MEM, D_MODEL), f32),
        "g_mix": gain(ks[4], (L, D_MODEL)),
        "w_in": w(ks[5], (L, D_MODEL, IN_WIDTH), D_MODEL),
        "w_fourier": w(ks[6], (L, F_GROUPS, F_CH, F_CH), F_CH),
        "g_q": gain(ks[7], (L, HEAD_DIM)),
        "g_k": gain(ks[8], (L, HEAD_DIM)),
        "w_out": w(ks[9], (L, MIX_WIDTH, D_MODEL), MIX_WIDTH, 0.5),
        "g_cross": gain(ks[10], (L, D_MODEL)),
        "g_mem": gain(ks[11], (L, D_MODEL)),
        "w_cq": w(ks[12], (L, D_MODEL, D_MODEL), D_MODEL),
        "w_ckv": w(ks[13], (L, D_MODEL, 2 * D_MODEL), D_MODEL),
        "g_cq": gain(ks[14], (L, X_HEAD_DIM)),
        "g_ck": gain(ks[15], (L, X_HEAD_DIM)),
        "w_co": w(ks[16], (L, D_MODEL, D_MODEL), D_MODEL, 0.5),
        "g_mlp": gain(ks[17], (L, D_MODEL)),
        "w_up": w(ks[18], (L, D_MODEL, D_FF), D_MODEL),
        "w_down": w(ks[19], (L, D_FF, D_MODEL), D_FF, 0.5),
    }


def reference(x_prompt, x_sample, mem_prompt, mem_sample, g_mix, w_in, w_fourier, g_q, g_k, w_out,
              g_cross, g_mem, w_cq, w_ckv, g_cq, g_ck, w_co, g_mlp, w_up, w_down):
    ang_p = axial_angles(x_prompt.shape[1])
    ang_s = axial_angles(x_sample.shape[1])
    yp = x_prompt
    ys = x_sample
    for l in range(DEPTH):
        p = (g_mix[l], w_in[l], w_fourier[l], g_q[l], g_k[l], w_out[l], g_cross[l], g_mem[l],
             w_cq[l], w_ckv[l], g_cq[l], g_ck[l], w_co[l], g_mlp[l], w_up[l], w_down[l])
        yp = encoder_layer(yp, mem_prompt, ang_p, *p)
        ys = encoder_layer(ys, mem_sample, ang_s, *p)
    return (yp, ys)
```

```python
import functools
import math

import numpy as np
import jax
import jax.numpy as jnp
from jax import lax
from jax.experimental import pallas as pl
from jax.experimental.pallas import tpu as pltpu

F32 = jnp.float32
BF16 = jnp.bfloat16

GRID_W = 64
F_GROUPS = 4
F_CH = 64
F_WIDTH = F_GROUPS * F_CH
HEAD_DIM = 128
N_Q_HEADS = 6
N_KV_HEADS = 2
Q_PER_KV = N_Q_HEADS // N_KV_HEADS
Q_WIDTH = N_Q_HEADS * HEAD_DIM
KV_WIDTH = N_KV_HEADS * HEAD_DIM
ROPE_AXIS_DIM = HEAD_DIM // 2
ROPE_THETA = 10000.0
X_HEADS = 4
EPS = 1e-6
LOG2E = 1.4426950408889634

VMEM_LIMIT_BYTES = 56 * 1024 * 1024
FFT_N2 = 128


def _params(*sem):
    return pltpu.CompilerParams(dimension_semantics=sem, vmem_limit_bytes=VMEM_LIMIT_BYTES)


def _const_spec(shape):
    nd = len(shape)
    return pl.BlockSpec(shape, lambda *_: (0,) * nd, pipeline_mode=pl.Buffered(1))


def _rms(x, g):
    ms = jnp.mean(x * x, axis=-1, keepdims=True)
    return x * lax.rsqrt(ms + EPS) * g


def _rope_tables(seq):
    rows = seq // GRID_W
    inv_freq = 1.0 / (ROPE_THETA ** (jnp.arange(0, ROPE_AXIS_DIM, 2, dtype=F32) / ROPE_AXIS_DIM))
    ang_r = jnp.arange(rows, dtype=F32)[:, None] * inv_freq[None, :]
    ang_c = jnp.arange(GRID_W, dtype=F32)[:, None] * inv_freq[None, :]

    def expand_r(a):
        return jnp.repeat(a, GRID_W, axis=0)

    def expand_c(a):
        return jnp.tile(a, (rows, 1))

    cr, sr = expand_r(jnp.cos(ang_r)), expand_r(jnp.sin(ang_r))
    cc, sc = expand_c(jnp.cos(ang_c)), expand_c(jnp.sin(ang_c))
    cos = jnp.concatenate([cr, cr, cc, cc], axis=-1)
    sin = jnp.concatenate([-sr, sr, -sc, sc], axis=-1)
    return cos, sin


def _channel_dft_table():
    c = np.arange(F_CH)
    ang = 2.0 * np.pi * np.outer(c, c) / F_CH
    cs, sn = np.cos(ang) / math.sqrt(F_CH), np.sin(ang) / math.sqrt(F_CH)
    t = np.zeros((F_WIDTH, 2 * F_WIDTH), np.float64)
    for g in range(F_GROUPS):
        sl = slice(g * F_CH, (g + 1) * F_CH)
        t[sl, sl] = cs
        t[sl, F_WIDTH + g * F_CH:F_WIDTH + (g + 1) * F_CH] = -sn
    return jnp.asarray(t, dtype=F32).astype(BF16)


def _stage1_table(n1):
    k = np.arange(n1)
    ang = 2.0 * np.pi * np.outer(k, k) / n1
    cs, sn = np.cos(ang) / math.sqrt(n1), np.sin(ang) / math.sqrt(n1)
    return jnp.asarray(np.block([[cs, sn], [-sn, cs]]), dtype=F32).astype(BF16)


def _stage2_tables(n1):
    n2 = FFT_N2
    n = n1 * n2
    alpha = 2.0 * np.pi * np.outer(np.arange(n1), np.arange(n2)) / n
    beta = 2.0 * np.pi * np.outer(np.arange(n2), np.arange(n2)) / n2
    s = 1.0 / math.sqrt(n2)
    as_f32 = lambda a: jnp.asarray(a, dtype=F32)
    return as_f32(np.cos(alpha)), as_f32(np.sin(alpha)), as_f32(np.cos(beta) * s), as_f32(np.sin(beta) * s)


def _in_proj_kernel(x_ref, gmix_ref, win_ref, dftc_ref, gq_ref, gk_ref, cos_ref, sin_ref,
                    ur_ref, ui_ref, qt_ref, k_ref, vt_ref, *, q_scale):
    tm = x_ref.shape[0]
    h = _rms(x_ref[...], gmix_ref[...]).astype(BF16)
    z = jnp.dot(h, win_ref[...], preferred_element_type=F32)

    u = jnp.dot(z[:, :F_WIDTH].astype(BF16), dftc_ref[...], preferred_element_type=F32)
    ur_ref[...] = u[:, :F_WIDTH].astype(BF16)
    ui_ref[...] = u[:, F_WIDTH:].astype(BF16)

    cos = cos_ref[...]
    sin = sin_ref[...]
    lane = lax.broadcasted_iota(jnp.int32, (tm, HEAD_DIM), 1)
    first_half = (lane % ROPE_AXIS_DIM) < (ROPE_AXIS_DIM // 2)

    def norm_rope(zh, g):
        y = _rms(zh, g)
        partner = jnp.where(first_half,
                            pltpu.roll(y, HEAD_DIM - ROPE_AXIS_DIM // 2, 1),
                            pltpu.roll(y, ROPE_AXIS_DIM // 2, 1))
        return y * cos + partner * sin

    gq = gq_ref[...]
    gk = gk_ref[...]
    for hh in range(N_Q_HEADS):
        lo = F_WIDTH + hh * HEAD_DIM
        q = norm_rope(z[:, lo:lo + HEAD_DIM], gq) * q_scale
        qt_ref[hh * HEAD_DIM:(hh + 1) * HEAD_DIM, :] = q.T.astype(BF16)
    for hh in range(N_KV_HEADS):
        lo = F_WIDTH + Q_WIDTH + hh * HEAD_DIM
        k_ref[:, hh * HEAD_DIM:(hh + 1) * HEAD_DIM] = norm_rope(z[:, lo:lo + HEAD_DIM], gk).astype(BF16)
        lo = F_WIDTH + Q_WIDTH + KV_WIDTH + hh * HEAD_DIM
        vt_ref[hh * HEAD_DIM:(hh + 1) * HEAD_DIM, :] = z[:, lo:lo + HEAD_DIM].T.astype(BF16)


def _in_proj(x2d, batch, seq, g_mix, w_in, dftc, g_q, g_k, cos, sin, tm=512):
    t, d = x2d.shape
    nb = seq // tm
    in_w = w_in.shape[1]
    row = lambda i: (i, 0)
    pos = lambda i: (i % nb, 0)
    tr = lambda i: (i // nb, 0, i % nb)
    q_scale = LOG2E / math.sqrt(HEAD_DIM)
    return pl.pallas_call(
        functools.partial(_in_proj_kernel, q_scale=q_scale),
        grid=(t // tm,),
        in_specs=[
            pl.BlockSpec((tm, d), row),
            _const_spec((1, d)),
            _const_spec((d, in_w)),
            _const_spec((F_WIDTH, 2 * F_WIDTH)),
            _const_spec((1, HEAD_DIM)),
            _const_spec((1, HEAD_DIM)),
            pl.BlockSpec((tm, HEAD_DIM), pos),
            pl.BlockSpec((tm, HEAD_DIM), pos),
        ],
        out_specs=[
            pl.BlockSpec((tm, F_WIDTH), row),
            pl.BlockSpec((tm, F_WIDTH), row),
            pl.BlockSpec((None, Q_WIDTH, tm), tr),
            pl.BlockSpec((tm, KV_WIDTH), row),
            pl.BlockSpec((None, KV_WIDTH, tm), tr),
        ],
        out_shape=[
            jax.ShapeDtypeStruct((t, F_WIDTH), BF16),
            jax.ShapeDtypeStruct((t, F_WIDTH), BF16),
            jax.ShapeDtypeStruct((batch, Q_WIDTH, seq), BF16),
            jax.ShapeDtypeStruct((t, KV_WIDTH), BF16),
            jax.ShapeDtypeStruct((batch, KV_WIDTH, seq), BF16),
        ],
        compiler_params=_params("parallel"),
        name="in_proj",
    )(x2d, g_mix, w_in, dftc, g_q, g_k, cos, sin)


def _fourier1_kernel(m1_ref, ur_ref, ui_ref, yr_ref, yi_ref):
    n1 = ur_ref.shape[0]
    u = jnp.concatenate([ur_ref[...], ui_ref[...]], axis=0)
    y = jnp.dot(m1_ref[...], u, preferred_element_type=F32)
    yr_ref[...] = y[:n1].astype(BF16)
    yi_ref[...] = y[n1:].astype(BF16)


def _fourier2_kernel(ca_ref, sa_ref, cb_ref, sb_ref, wbd_ref, yr_ref, yi_ref, o_ref, *, kb):
    n2 = FFT_N2
    g = pl.program_id(1)
    cb = cb_ref[...]
    sb = sb_ref[...]
    wbd = wbd_ref[...]
    for j in range(kb):
        k1 = g * kb + j
        ca = ca_ref[pl.ds(k1, 1), :]
        sa = sa_ref[pl.ds(k1, 1), :]
        tw = jnp.concatenate([cb * ca - sb * sa, sb * ca + cb * sa], axis=1).astype(BF16)
        yy = jnp.concatenate([yr_ref[j * n2:(j + 1) * n2, :], yi_ref[j * n2:(j + 1) * n2, :]], axis=0)
        fr = jnp.dot(tw, yy, preferred_element_type=F32)
        o = jnp.dot(fr.astype(BF16), wbd, preferred_element_type=F32)
        o_ref[:, j * F_WIDTH:(j + 1) * F_WIDTH] = o.astype(BF16)


def _fourier(ur, ui, batch, seq, wbd, kb=8):
    n2 = FFT_N2
    n1 = seq // n2
    cols = n2 * F_WIDTH
    tc = min(cols, (4 * 1024 * 1024) // (2 * n1 * 4))
    m1 = _stage1_table(n1)
    ca, sa, cb, sb = _stage2_tables(n1)
    urv = ur.reshape(batch, n1, cols)
    uiv = ui.reshape(batch, n1, cols)
    blk1 = pl.BlockSpec((None, n1, tc), lambda b, c: (b, 0, c))
    yr, yi = pl.pallas_call(
        _fourier1_kernel,
        grid=(batch, cols // tc),
        in_specs=[_const_spec((2 * n1, 2 * n1)), blk1, blk1],
        out_specs=[blk1, blk1],
        out_shape=[jax.ShapeDtypeStruct((batch, n1, cols), BF16)] * 2,
        compiler_params=_params("parallel", "parallel"),
        name="fourier_stage1",
    )(m1, urv, uiv)

    kb = min(kb, n1)
    yblk = pl.BlockSpec((None, kb * n2, F_WIDTH), lambda b, g: (b, g, 0))
    out = pl.pallas_call(
        functools.partial(_fourier2_kernel, kb=kb),
        grid=(batch, n1 // kb),
        in_specs=[_const_spec((n1, n2)), _const_spec((n1, n2)), _const_spec((n2, n2)), _const_spec((n2, n2)),
                  _const_spec((F_WIDTH, F_WIDTH)), yblk, yblk],
        out_specs=pl.BlockSpec((None, n2, kb * F_WIDTH), lambda b, g: (b, 0, g)),
        out_shape=jax.ShapeDtypeStruct((batch, n2, n1 * F_WIDTH), BF16),
        compiler_params=_params("parallel", "parallel"),
        name="fourier_stage2",
    )(ca, sa, cb, sb, wbd, yr.reshape(batch, seq, F_WIDTH), yi.reshape(batch, seq, F_WIDTH))
    return out.reshape(batch * seq, F_WIDTH)


def _attention_kernel(qt_ref, k_ref, vt_ref, o_ref, m_ref, l_ref, acc_ref):
    j = pl.program_id(3)

    @pl.when(j == 0)
    def _():
        m_ref[...] = jnp.full(m_ref.shape, -jnp.inf, F32)
        l_ref[...] = jnp.zeros(l_ref.shape, F32)
        acc_ref[...] = jnp.zeros(acc_ref.shape, F32)

    k = k_ref[...]
    vt = vt_ref[...]
    for g in range(Q_PER_KV):
        qt = qt_ref[g * HEAD_DIM:(g + 1) * HEAD_DIM, :]
        s = jnp.dot(k, qt, preferred_element_type=F32)
        m_prev = m_ref[g]
        m_new = jnp.maximum(m_prev, jnp.max(s, axis=0, keepdims=True))
        alpha = jnp.exp2(m_prev - m_new)
        p = jnp.exp2(s - m_new)
        l_ref[g] = alpha * l_ref[g] + jnp.sum(p, axis=0, keepdims=True)
        acc_ref[g] = alpha * acc_ref[g] + jnp.dot(vt, p.astype(BF16), preferred_element_type=F32)
        m_ref[g] = m_new

    @pl.when(j == pl.num_programs(3) - 1)
    def _():
        for g in range(Q_PER_KV):
            o = acc_ref[g] * (1.0 / l_ref[g])
            o_ref[:, g * HEAD_DIM:(g + 1) * HEAD_DIM] = o.T.astype(BF16)


def _attention(qt, k, vt, tq=512, tk=512):
    batch, _, seq = qt.shape
    gw = Q_PER_KV * HEAD_DIM
    return pl.pallas_call(
        _attention_kernel,
        grid=(batch, N_KV_HEADS, seq // tq, seq // tk),
        in_specs=[
            pl.BlockSpec((None, gw, tq), lambda b, h, i, j: (b, h, i)),
            pl.BlockSpec((None, tk, HEAD_DIM), lambda b, h, i, j: (b, j, h)),
            pl.BlockSpec((None, HEAD_DIM, tk), lambda b, h, i, j: (b, h, j)),
        ],
        out_specs=pl.BlockSpec((None, tq, gw), lambda b, h, i, j: (b, i, h)),
        out_shape=jax.ShapeDtypeStruct((batch, seq, Q_WIDTH), BF16),
        scratch_shapes=[
            pltpu.VMEM((Q_PER_KV, 1, tq), F32),
            pltpu.VMEM((Q_PER_KV, 1, tq), F32),
            pltpu.VMEM((Q_PER_KV, HEAD_DIM, tq), F32),
        ],
        compiler_params=_params("parallel", "parallel", "parallel", "arbitrary"),
        name="self_attention",
    )(qt, k, vt)


def _mem_kv_kernel(mem_ref, gmem_ref, wckv_ref, gck_ref, kt_ref, v_ref):
    d = mem_ref.shape[1]
    xh = d // X_HEADS
    h = _rms(mem_ref[...], gmem_ref[...]).astype(BF16)
    kv = jnp.dot(h, wckv_ref[...], preferred_element_type=F32)
    gck = gck_ref[...]
    for hh in range(X_HEADS):
        kh = _rms(kv[:, hh * xh:(hh + 1) * xh], gck)
        kt_ref[hh * xh:(hh + 1) * xh, :] = kh.T.astype(BF16)
    v_ref[...] = kv[:, d:].astype(BF16)


def _mem_kv(mem, g_mem, w_ckv, g_ck):
    batch, m, d = mem.shape
    return pl.pallas_call(
        _mem_kv_kernel,
        grid=(batch,),
        in_specs=[pl.BlockSpec((None, m, d), lambda b: (b, 0, 0)),
                  _const_spec((1, d)), _const_spec((d, 2 * d)), _const_spec((1, d // X_HEADS))],
        out_specs=[pl.BlockSpec((None, d, m), lambda b: (b, 0, 0)),
                   pl.BlockSpec((None, m, d), lambda b: (b, 0, 0))],
        out_shape=[jax.ShapeDtypeStruct((batch, d, m), BF16), jax.ShapeDtypeStruct((batch, m, d), BF16)],
        compiler_params=_params("parallel"),
        name="mem_kv",
    )(mem, g_mem, w_ckv, g_ck)


def _post_kernel(x_ref, f_ref, a_ref, wout_ref, gcross_ref, wcq_ref, gcq_ref, kt_ref, v_ref, wco_ref,
                 gmlp_ref, wup_ref, wdown_ref, y_ref, *, x_scale, ff_chunk):
    d = x_ref.shape[1]
    xh = d // X_HEADS
    x1 = (x_ref[...]
          + jnp.dot(f_ref[...], wout_ref[:F_WIDTH, :], preferred_element_type=F32)
          + jnp.dot(a_ref[...], wout_ref[F_WIDTH:, :], preferred_element_type=F32))

    qc = jnp.dot(_rms(x1, gcross_ref[...]).astype(BF16), wcq_ref[...], preferred_element_type=F32)
    gcq = gcq_ref[...]
    heads = []
    for hh in range(X_HEADS):
        sl = slice(hh * xh, (hh + 1) * xh)
        qh = (_rms(qc[:, sl], gcq) * x_scale).astype(BF16)
        s = jnp.dot(qh, kt_ref[sl, :], preferred_element_type=F32)
        p = jnp.exp2(s - jnp.max(s, axis=-1, keepdims=True))
        o = jnp.dot(p.astype(BF16), v_ref[:, sl], preferred_element_type=F32)
        heads.append((o * (1.0 / jnp.sum(p, axis=-1, keepdims=True))).astype(BF16))
    x2 = x1 + jnp.dot(jnp.concatenate(heads, axis=-1), wco_ref[...], preferred_element_type=F32)

    h3 = _rms(x2, gmlp_ref[...]).astype(BF16)
    acc = x2
    for c in range(wup_ref.shape[1] // ff_chunk):
        sl = slice(c * ff_chunk, (c + 1) * ff_chunk)
        u = jnp.maximum(jnp.dot(h3, wup_ref[:, sl], preferred_element_type=F32), 0.0)
        acc = acc + jnp.dot((u * u).astype(BF16), wdown_ref[sl, :], preferred_element_type=F32)
    y_ref[...] = acc


def _post(x2d, f_out, a_out, kt, v, seq, w_out, g_cross, w_cq, g_cq, w_co, g_mlp, w_up, w_down, tm=256):
    t, d = x2d.shape
    nb = seq // tm
    m = v.shape[1]
    d_ff = w_up.shape[1]
    row = lambda i: (i, 0)
    x_scale = LOG2E / math.sqrt(d // X_HEADS)
    return pl.pallas_call(
        functools.partial(_post_kernel, x_scale=x_scale, ff_chunk=1024),
        grid=(t // tm,),
        in_specs=[
            pl.BlockSpec((tm, d), row),
            pl.BlockSpec((tm, F_WIDTH), row),
            pl.BlockSpec((tm, Q_WIDTH), row),
            _const_spec((d, d)),
            _const_spec((1, d)),
            _const_spec((d, d)),
            _const_spec((1, d // X_HEADS)),
            pl.BlockSpec((None, d, m), lambda i: (i // nb, 0, 0)),
            pl.BlockSpec((None, m, d), lambda i: (i // nb, 0, 0)),
            _const_spec((d, d)),
            _const_spec((1, d)),
            _const_spec((d, d_ff)),
            _const_spec((d_ff, d)),
        ],
        out_specs=pl.BlockSpec((tm, d), row),
        out_shape=jax.ShapeDtypeStruct((t, d), F32),
        compiler_params=_params("parallel"),
        name="post",
    )(x2d, f_out, a_out, w_out, g_cross, w_cq, g_cq, kt, v, w_co, g_mlp, w_up, w_down)


def _block_diag_weights(w_f):
    out = jnp.zeros((F_WIDTH, F_WIDTH), w_f.dtype)
    for g in range(F_GROUPS):
        out = out.at[g * F_CH:(g + 1) * F_CH, g * F_CH:(g + 1) * F_CH].set(w_f[g])
    return out


def _layer(x, mem, cos, sin, dftc, p):
    batch, seq, d = x.shape
    x2d = x.reshape(batch * seq, d)
    ur, ui, qt, k, vt = _in_proj(x2d, batch, seq, p["g_mix"], p["w_in"], dftc, p["g_q"], p["g_k"],
                                 cos[:seq], sin[:seq])
    f_out = _fourier(ur, ui, batch, seq, p["wbd"])
    a_out = _attention(qt, k.reshape(batch, seq, KV_WIDTH), vt).reshape(batch * seq, Q_WIDTH)
    kt, v = _mem_kv(mem, p["g_mem"], p["w_ckv"], p["g_ck"])
    y = _post(x2d, f_out, a_out, kt, v, seq, p["w_out"], p["g_cross"], p["w_cq"], p["g_cq"], p["w_co"],
              p["g_mlp"], p["w_up"], p["w_down"])
    return y.reshape(batch, seq, d)


def kernel(x_prompt, x_sample, mem_prompt, mem_sample, g_mix, w_in, w_fourier, g_q, g_k, w_out, g_cross, g_mem,
           w_cq, w_ckv, g_cq, g_ck, w_co, g_mlp, w_up, w_down):
    depth = w_in.shape[0]
    max_seq = max(x_prompt.shape[1], x_sample.shape[1])
    cos, sin = _rope_tables(max_seq)
    dftc = _channel_dft_table()
    yp, ys = x_prompt, x_sample
    for l in range(depth):
        row = lambda g: g[l][None, :]
        p = dict(
            g_mix=row(g_mix), w_in=w_in[l].astype(BF16), wbd=_block_diag_weights(w_fourier[l]).astype(BF16),
            g_q=row(g_q), g_k=row(g_k), w_out=w_out[l].astype(BF16), g_cross=row(g_cross), g_mem=row(g_mem),
            w_cq=w_cq[l].astype(BF16), w_ckv=w_ckv[l].astype(BF16), g_cq=row(g_cq), g_ck=row(g_ck),
            w_co=w_co[l].astype(BF16), g_mlp=row(g_mlp), w_up=w_up[l].astype(BF16), w_down=w_down[l].astype(BF16),
        )
        yp = _layer(yp, mem_prompt, cos, sin, dftc, p)
        ys = _layer(ys, mem_sample, cos, sin, dftc, p)
    return (yp, ys)
```

```python
import functools
import math

import numpy as np
import jax
import jax.numpy as jnp
from jax import lax
from jax.experimental import pallas as pl
from jax.experimental.pallas import tpu as pltpu

F32 = jnp.float32
BF16 = jnp.bfloat16

GRID_W = 64
F_GROUPS = 4
F_CH = 64
F_WIDTH = F_GROUPS * F_CH
HEAD_DIM = 128
N_Q_HEADS = 6
N_KV_HEADS = 2
Q_PER_KV = N_Q_HEADS // N_KV_HEADS
Q_WIDTH = N_Q_HEADS * HEAD_DIM
KV_WIDTH = N_KV_HEADS * HEAD_DIM
ROPE_AXIS_DIM = HEAD_DIM // 2
ROPE_THETA = 10000.0
X_HEADS = 4
EPS = 1e-6
LOG2E = 1.4426950408889634

VMEM_LIMIT_BYTES = 56 * 1024 * 1024
FFT_N2 = 128
SAFE_SCORE_BOUND = 60.0
DENOM_ROWS = 16


def _params(*sem):
    return pltpu.CompilerParams(dimension_semantics=sem, vmem_limit_bytes=VMEM_LIMIT_BYTES)


def _const_spec(shape):
    nd = len(shape)
    return pl.BlockSpec(shape, lambda *_: (0,) * nd, pipeline_mode=pl.Buffered(1))


def _rms(x, g):
    ms = jnp.mean(x * x, axis=-1, keepdims=True)
    return x * lax.rsqrt(ms + EPS) * g


def _rope_tables(seq):
    rows = seq // GRID_W
    inv_freq = 1.0 / (ROPE_THETA ** (jnp.arange(0, ROPE_AXIS_DIM, 2, dtype=F32) / ROPE_AXIS_DIM))
    ang_r = jnp.arange(rows, dtype=F32)[:, None] * inv_freq[None, :]
    ang_c = jnp.arange(GRID_W, dtype=F32)[:, None] * inv_freq[None, :]

    def expand_r(a):
        return jnp.repeat(a, GRID_W, axis=0)

    def expand_c(a):
        return jnp.tile(a, (rows, 1))

    cr, sr = expand_r(jnp.cos(ang_r)), expand_r(jnp.sin(ang_r))
    cc, sc = expand_c(jnp.cos(ang_c)), expand_c(jnp.sin(ang_c))
    cos = jnp.concatenate([cr, cr, cc, cc], axis=-1)
    sin = jnp.concatenate([-sr, sr, -sc, sc], axis=-1)
    return cos, sin


def _channel_dft_table():
    c = np.arange(F_CH)
    ang = 2.0 * np.pi * np.outer(c, c) / F_CH
    cs, sn = np.cos(ang) / math.sqrt(F_CH), np.sin(ang) / math.sqrt(F_CH)
    t = np.zeros((F_WIDTH, 2 * F_WIDTH), np.float64)
    for g in range(F_GROUPS):
        sl = slice(g * F_CH, (g + 1) * F_CH)
        t[sl, sl] = cs
        t[sl, F_WIDTH + g * F_CH:F_WIDTH + (g + 1) * F_CH] = -sn
    return jnp.asarray(t, dtype=F32).astype(BF16)


def _stage1_table(n1):
    k = np.arange(n1)
    ang = 2.0 * np.pi * np.outer(k, k) / n1
    cs, sn = np.cos(ang) / math.sqrt(n1), np.sin(ang) / math.sqrt(n1)
    return jnp.asarray(np.block([[cs, sn], [-sn, cs]]), dtype=F32).astype(BF16)


def _stage2_tables(n1):
    n2 = FFT_N2
    n = n1 * n2
    alpha = 2.0 * np.pi * np.outer(np.arange(n1), np.arange(n2)) / n
    beta = 2.0 * np.pi * np.outer(np.arange(n2), np.arange(n2)) / n2
    s = 1.0 / math.sqrt(n2)
    as_f32 = lambda a: jnp.asarray(a, dtype=F32)
    return as_f32(np.cos(alpha)), as_f32(np.sin(alpha)), as_f32(np.cos(beta) * s), as_f32(np.sin(beta) * s)


def _in_proj_kernel(x_ref, gmix_ref, win_ref, dftc_ref, gq_ref, gk_ref, cos_ref, sin_ref,
                    ur_ref, ui_ref, qt_ref, k_ref, vt_ref, *, q_scale):
    tm = x_ref.shape[0]
    h = _rms(x_ref[...], gmix_ref[...]).astype(BF16)
    z = jnp.dot(h, win_ref[...], preferred_element_type=F32)

    u = jnp.dot(z[:, :F_WIDTH].astype(BF16), dftc_ref[...], preferred_element_type=F32)
    ur_ref[...] = u[:, :F_WIDTH].astype(BF16)
    ui_ref[...] = u[:, F_WIDTH:].astype(BF16)

    cos = cos_ref[...]
    sin = sin_ref[...]
    lane = lax.broadcasted_iota(jnp.int32, (tm, HEAD_DIM), 1)
    first_half = (lane % ROPE_AXIS_DIM) < (ROPE_AXIS_DIM // 2)

    def norm_rope(zh, g):
        y = _rms(zh, g)
        partner = jnp.where(first_half,
                            pltpu.roll(y, HEAD_DIM - ROPE_AXIS_DIM // 2, 1),
                            pltpu.roll(y, ROPE_AXIS_DIM // 2, 1))
        return y * cos + partner * sin

    gq = gq_ref[...]
    gk = gk_ref[...]
    for hh in range(N_Q_HEADS):
        lo = F_WIDTH + hh * HEAD_DIM
        q = norm_rope(z[:, lo:lo + HEAD_DIM], gq) * q_scale
        qt_ref[hh * HEAD_DIM:(hh + 1) * HEAD_DIM, :] = q.T.astype(BF16)
    for hh in range(N_KV_HEADS):
        lo = F_WIDTH + Q_WIDTH + hh * HEAD_DIM
        k_ref[:, hh * HEAD_DIM:(hh + 1) * HEAD_DIM] = norm_rope(z[:, lo:lo + HEAD_DIM], gk).astype(BF16)
        lo = F_WIDTH + Q_WIDTH + KV_WIDTH + hh * HEAD_DIM
        vt_ref[hh * HEAD_DIM:(hh + 1) * HEAD_DIM, :] = z[:, lo:lo + HEAD_DIM].T.astype(BF16)


def _in_proj(x2d, batch, seq, g_mix, w_in, dftc, g_q, g_k, cos, sin, tm=512):
    t, d = x2d.shape
    nb = seq // tm
    in_w = w_in.shape[1]
    row = lambda i: (i, 0)
    pos = lambda i: (i % nb, 0)
    tr = lambda i: (i // nb, 0, i % nb)
    q_scale = LOG2E / math.sqrt(HEAD_DIM)
    return pl.pallas_call(
        functools.partial(_in_proj_kernel, q_scale=q_scale),
        grid=(t // tm,),
        in_specs=[
            pl.BlockSpec((tm, d), row),
            _const_spec((1, d)),
            _const_spec((d, in_w)),
            _const_spec((F_WIDTH, 2 * F_WIDTH)),
            _const_spec((1, HEAD_DIM)),
            _const_spec((1, HEAD_DIM)),
            pl.BlockSpec((tm, HEAD_DIM), pos),
            pl.BlockSpec((tm, HEAD_DIM), pos),
        ],
        out_specs=[
            pl.BlockSpec((tm, F_WIDTH), row),
            pl.BlockSpec((tm, F_WIDTH), row),
            pl.BlockSpec((None, Q_WIDTH, tm), tr),
            pl.BlockSpec((tm, KV_WIDTH), row),
            pl.BlockSpec((None, KV_WIDTH, tm), tr),
        ],
        out_shape=[
            jax.ShapeDtypeStruct((t, F_WIDTH), BF16),
            jax.ShapeDtypeStruct((t, F_WIDTH), BF16),
            jax.ShapeDtypeStruct((batch, Q_WIDTH, seq), BF16),
            jax.ShapeDtypeStruct((t, KV_WIDTH), BF16),
            jax.ShapeDtypeStruct((batch, KV_WIDTH, seq), BF16),
        ],
        compiler_params=_params("parallel"),
        name="in_proj",
    )(x2d, g_mix, w_in, dftc, g_q, g_k, cos, sin)


def _fourier1_kernel(m1_ref, ur_ref, ui_ref, yr_ref, yi_ref):
    n1 = ur_ref.shape[0]
    u = jnp.concatenate([ur_ref[...], ui_ref[...]], axis=0)
    y = jnp.dot(m1_ref[...], u, preferred_element_type=F32)
    yr_ref[...] = y[:n1].astype(BF16)
    yi_ref[...] = y[n1:].astype(BF16)


def _fourier2_kernel(ca_ref, sa_ref, cb_ref, sb_ref, wbd_ref, yr_ref, yi_ref, o_ref, *, kb):
    n2 = FFT_N2
    g = pl.program_id(1)
    cb = cb_ref[...]
    sb = sb_ref[...]
    wbd = wbd_ref[...]
    for j in range(kb):
        k1 = g * kb + j
        ca = ca_ref[pl.ds(k1, 1), :]
        sa = sa_ref[pl.ds(k1, 1), :]
        tw = jnp.concatenate([cb * ca - sb * sa, sb * ca + cb * sa], axis=1).astype(BF16)
        yy = jnp.concatenate([yr_ref[j * n2:(j + 1) * n2, :], yi_ref[j * n2:(j + 1) * n2, :]], axis=0)
        fr = jnp.dot(tw, yy, preferred_element_type=F32)
        o = jnp.dot(fr.astype(BF16), wbd, preferred_element_type=F32)
        o_ref[:, j * F_WIDTH:(j + 1) * F_WIDTH] = o.astype(BF16)


def _fourier(ur, ui, batch, seq, wbd, kb=8):
    n2 = FFT_N2
    n1 = seq // n2
    cols = n2 * F_WIDTH
    tc = min(cols, (4 * 1024 * 1024) // (2 * n1 * 4))
    m1 = _stage1_table(n1)
    ca, sa, cb, sb = _stage2_tables(n1)
    urv = ur.reshape(batch, n1, cols)
    uiv = ui.reshape(batch, n1, cols)
    blk1 = pl.BlockSpec((None, n1, tc), lambda b, c: (b, 0, c))
    yr, yi = pl.pallas_call(
        _fourier1_kernel,
        grid=(batch, cols // tc),
        in_specs=[_const_spec((2 * n1, 2 * n1)), blk1, blk1],
        out_specs=[blk1, blk1],
        out_shape=[jax.ShapeDtypeStruct((batch, n1, cols), BF16)] * 2,
        compiler_params=_params("parallel", "parallel"),
        name="fourier_stage1",
    )(m1, urv, uiv)

    kb = min(kb, n1)
    yblk = pl.BlockSpec((None, kb * n2, F_WIDTH), lambda b, g: (b, g, 0))
    out = pl.pallas_call(
        functools.partial(_fourier2_kernel, kb=kb),
        grid=(batch, n1 // kb),
        in_specs=[_const_spec((n1, n2)), _const_spec((n1, n2)), _const_spec((n2, n2)), _const_spec((n2, n2)),
                  _const_spec((F_WIDTH, F_WIDTH)), yblk, yblk],
        out_specs=pl.BlockSpec((None, n2, kb * F_WIDTH), lambda b, g: (b, 0, g)),
        out_shape=jax.ShapeDtypeStruct((batch, n2, n1 * F_WIDTH), BF16),
        compiler_params=_params("parallel", "parallel"),
        name="fourier_stage2",
    )(ca, sa, cb, sb, wbd, yr.reshape(batch, seq, F_WIDTH), yi.reshape(batch, seq, F_WIDTH))
    return out.reshape(batch * seq, F_WIDTH)


def _attention_kernel(qt_ref, k_ref, vt_ref, o_ref, m_ref, l_ref, acc_ref):
    j = pl.program_id(3)

    @pl.when(j == 0)
    def _():
        m_ref[...] = jnp.full(m_ref.shape, -jnp.inf, F32)
        l_ref[...] = jnp.zeros(l_ref.shape, F32)
        acc_ref[...] = jnp.zeros(acc_ref.shape, F32)

    k = k_ref[...]
    vt = vt_ref[...]
    for g in range(Q_PER_KV):
        qt = qt_ref[g * HEAD_DIM:(g + 1) * HEAD_DIM, :]
        s = jnp.dot(k, qt, preferred_element_type=F32)
        m_prev = m_ref[g]
        m_new = jnp.maximum(m_prev, jnp.max(s, axis=0, keepdims=True))
        alpha = jnp.exp2(m_prev - m_new)
        p = jnp.exp2(s - m_new)
        l_ref[g] = alpha * l_ref[g] + jnp.sum(p, axis=0, keepdims=True)
        acc_ref[g] = alpha * acc_ref[g] + jnp.dot(vt, p.astype(BF16), preferred_element_type=F32)
        m_ref[g] = m_new

    @pl.when(j == pl.num_programs(3) - 1)
    def _():
        for g in range(Q_PER_KV):
            o = acc_ref[g] * (1.0 / l_ref[g])
            o_ref[:, g * HEAD_DIM:(g + 1) * HEAD_DIM] = o.T.astype(BF16)


def _attention_bounded_kernel(qt_ref, k_ref, vt_ref, o_ref, acc_ref):
    j = pl.program_id(3)

    @pl.when(j == 0)
    def _():
        acc_ref[...] = jnp.zeros(acc_ref.shape, F32)

    k = k_ref[...]
    tk = k.shape[0]
    vt_aug = jnp.concatenate([vt_ref[...], jnp.ones((DENOM_ROWS, tk), BF16)], axis=0)
    for g in range(Q_PER_KV):
        s = jnp.dot(k, qt_ref[g * HEAD_DIM:(g + 1) * HEAD_DIM, :], preferred_element_type=F32)
        acc_ref[g] += jnp.dot(vt_aug, jnp.exp2(s).astype(BF16), preferred_element_type=F32)

    @pl.when(j == pl.num_programs(3) - 1)
    def _():
        for g in range(Q_PER_KV):
            a = acc_ref[g]
            o = a[:HEAD_DIM] * (1.0 / a[HEAD_DIM:HEAD_DIM + 1])
            o_ref[:, g * HEAD_DIM:(g + 1) * HEAD_DIM] = o.T.astype(BF16)


def _attention_call(body, scratch, qt, k, vt, tq, tk, name):
    batch, _, seq = qt.shape
    tq, tk = min(tq, seq), min(tk, seq)
    gw = Q_PER_KV * HEAD_DIM
    return pl.pallas_call(
        body,
        grid=(batch, N_KV_HEADS, seq // tq, seq // tk),
        in_specs=[
            pl.BlockSpec((None, gw, tq), lambda b, h, i, j: (b, h, i)),
            pl.BlockSpec((None, tk, HEAD_DIM), lambda b, h, i, j: (b, j, h)),
            pl.BlockSpec((None, HEAD_DIM, tk), lambda b, h, i, j: (b, h, j)),
        ],
        out_specs=pl.BlockSpec((None, tq, gw), lambda b, h, i, j: (b, i, h)),
        out_shape=jax.ShapeDtypeStruct((batch, seq, Q_WIDTH), BF16),
        scratch_shapes=scratch(tq),
        compiler_params=_params("parallel", "parallel", "parallel", "arbitrary"),
        name=name,
    )(qt, k, vt)


def _attention(qt, k, vt, score_bound):
    online_scratch = lambda tq: [pltpu.VMEM((Q_PER_KV, 1, tq), F32), pltpu.VMEM((Q_PER_KV, 1, tq), F32),
                                 pltpu.VMEM((Q_PER_KV, HEAD_DIM, tq), F32)]
    bounded_scratch = lambda tq: [pltpu.VMEM((Q_PER_KV, HEAD_DIM + DENOM_ROWS, tq), F32)]
    bounded = functools.partial(_attention_call, _attention_bounded_kernel, bounded_scratch,
                                tq=1024, tk=1024, name="self_attention_bounded")
    online = functools.partial(_attention_call, _attention_kernel, online_scratch,
                               tq=512, tk=512, name="self_attention_online")
    return lax.cond(score_bound <= SAFE_SCORE_BOUND, bounded, online, qt, k, vt)


def _mem_kv_kernel(mem_ref, gmem_ref, wckv_ref, gck_ref, kt_ref, v_ref):
    d = mem_ref.shape[1]
    xh = d // X_HEADS
    h = _rms(mem_ref[...], gmem_ref[...]).astype(BF16)
    kv = jnp.dot(h, wckv_ref[...], preferred_element_type=F32)
    gck = gck_ref[...]
    for hh in range(X_HEADS):
        kh = _rms(kv[:, hh * xh:(hh + 1) * xh], gck)
        kt_ref[hh * xh:(hh + 1) * xh, :] = kh.T.astype(BF16)
    v_ref[...] = kv[:, d:].astype(BF16)


def _mem_kv(mem, g_mem, w_ckv, g_ck):
    batch, m, d = mem.shape
    return pl.pallas_call(
        _mem_kv_kernel,
        grid=(batch,),
        in_specs=[pl.BlockSpec((None, m, d), lambda b: (b, 0, 0)),
                  _const_spec((1, d)), _const_spec((d, 2 * d)), _const_spec((1, d // X_HEADS))],
        out_specs=[pl.BlockSpec((None, d, m), lambda b: (b, 0, 0)),
                   pl.BlockSpec((None, m, d), lambda b: (b, 0, 0))],
        out_shape=[jax.ShapeDtypeStruct((batch, d, m), BF16), jax.ShapeDtypeStruct((batch, m, d), BF16)],
        compiler_params=_params("parallel"),
        name="mem_kv",
    )(mem, g_mem, w_ckv, g_ck)


def _post_kernel(x_ref, f_ref, a_ref, wout_ref, gcross_ref, wcq_ref, gcq_ref, kt_ref, v_ref, wco_ref,
                 gmlp_ref, wup_ref, wdown_ref, y_ref, *, x_scale, ff_chunk):
    d = x_ref.shape[1]
    xh = d // X_HEADS
    x1 = (x_ref[...]
          + jnp.dot(f_ref[...], wout_ref[:F_WIDTH, :], preferred_element_type=F32)
          + jnp.dot(a_ref[...], wout_ref[F_WIDTH:, :], preferred_element_type=F32))

    qc = jnp.dot(_rms(x1, gcross_ref[...]).astype(BF16), wcq_ref[...], preferred_element_type=F32)
    gcq = gcq_ref[...]
    heads = []
    for hh in range(X_HEADS):
        sl = slice(hh * xh, (hh + 1) * xh)
        qh = (_rms(qc[:, sl], gcq) * x_scale).astype(BF16)
        s = jnp.dot(qh, kt_ref[sl, :], preferred_element_type=F32)
        p = jnp.exp2(s - jnp.max(s, axis=-1, keepdims=True))
        o = jnp.dot(p.astype(BF16), v_ref[:, sl], preferred_element_type=F32)
        heads.append((o * (1.0 / jnp.sum(p, axis=-1, keepdims=True))).astype(BF16))
    x2 = x1 + jnp.dot(jnp.concatenate(heads, axis=-1), wco_ref[...], preferred_element_type=F32)

    h3 = _rms(x2, gmlp_ref[...]).astype(BF16)
    acc = x2
    for c in range(wup_ref.shape[1] // ff_chunk):
        sl = slice(c * ff_chunk, (c + 1) * ff_chunk)
        u = jnp.maximum(jnp.dot(h3, wup_ref[:, sl], preferred_element_type=F32), 0.0)
        acc = acc + jnp.dot((u * u).astype(BF16), wdown_ref[sl, :], preferred_element_type=F32)
    y_ref[...] = acc


def _post(x2d, f_out, a_out, kt, v, seq, w_out, g_cross, w_cq, g_cq, w_co, g_mlp, w_up, w_down, tm=256):
    t, d = x2d.shape
    nb = seq // tm
    m = v.shape[1]
    d_ff = w_up.shape[1]
    row = lambda i: (i, 0)
    x_scale = LOG2E / math.sqrt(d // X_HEADS)
    return pl.pallas_call(
        functools.partial(_post_kernel, x_scale=x_scale, ff_chunk=1024),
        grid=(t // tm,),
        in_specs=[
            pl.BlockSpec((tm, d), row),
            pl.BlockSpec((tm, F_WIDTH), row),
            pl.BlockSpec((tm, Q_WIDTH), row),
            _const_spec((d, d)),
            _const_spec((1, d)),
            _const_spec((d, d)),
            _const_spec((1, d // X_HEADS)),
            pl.BlockSpec((None, d, m), lambda i: (i // nb, 0, 0)),
            pl.BlockSpec((None, m, d), lambda i: (i // nb, 0, 0)),
            _const_spec((d, d)),
            _const_spec((1, d)),
            _const_spec((d, d_ff)),
            _const_spec((d_ff, d)),
        ],
        out_specs=pl.BlockSpec((tm, d), row),
        out_shape=jax.ShapeDtypeStruct((t, d), F32),
        compiler_params=_params("parallel"),
        name="post",
    )(x2d, f_out, a_out, w_out, g_cross, w_cq, g_cq, kt, v, w_co, g_mlp, w_up, w_down)


def _block_diag_weights(w_f):
    out = jnp.zeros((F_WIDTH, F_WIDTH), w_f.dtype)
    for g in range(F_GROUPS):
        out = out.at[g * F_CH:(g + 1) * F_CH, g * F_CH:(g + 1) * F_CH].set(w_f[g])
    return out


def _layer(x, mem, cos, sin, dftc, p):
    batch, seq, d = x.shape
    x2d = x.reshape(batch * seq, d)
    ur, ui, qt, k, vt = _in_proj(x2d, batch, seq, p["g_mix"], p["w_in"], dftc, p["g_q"], p["g_k"],
                                 cos[:seq], sin[:seq])
    f_out = _fourier(ur, ui, batch, seq, p["wbd"])
    score_bound = (HEAD_DIM * LOG2E / math.sqrt(HEAD_DIM)) * jnp.max(jnp.abs(p["g_q"])) * jnp.max(jnp.abs(p["g_k"]))
    a_out = _attention(qt, k.reshape(batch, seq, KV_WIDTH), vt, score_bound).reshape(batch * seq, Q_WIDTH)
    kt, v = _mem_kv(mem, p["g_mem"], p["w_ckv"], p["g_ck"])
    y = _post(x2d, f_out, a_out, kt, v, seq, p["w_out"], p["g_cross"], p["w_cq"], p["g_cq"], p["w_co"],
              p["g_mlp"], p["w_up"], p["w_down"])
    return y.reshape(batch, seq, d)


def kernel(x_prompt, x_sample, mem_prompt, mem_sample, g_mix, w_in, w_fourier, g_q, g_k, w_out, g_cross, g_mem,
           w_cq, w_ckv, g_cq, g_ck, w_co, g_mlp, w_up, w_down):
    depth = w_in.shape[0]
    max_seq = max(x_prompt.shape[1], x_sample.shape[1])
    cos, sin = _rope_tables(max_seq)
    dftc = _channel_dft_table()
    yp, ys = x_prompt, x_sample
    for l in range(depth):
        row = lambda g: g[l][None, :]
        p = dict(
            g_mix=row(g_mix), w_in=w_in[l].astype(BF16), wbd=_block_diag_weights(w_fourier[l]).astype(BF16),
            g_q=row(g_q), g_k=row(g_k), w_out=w_out[l].astype(BF16), g_cross=row(g_cross), g_mem=row(g_mem),
            w_cq=w_cq[l].astype(BF16), w_ckv=w_ckv[l].astype(BF16), g_cq=row(g_cq), g_ck=row(g_ck),
            w_co=w_co[l].astype(BF16), g_mlp=row(g_mlp), w_up=w_up[l].astype(BF16), w_down=w_down[l].astype(BF16),
        )
        yp = _layer(yp, mem_prompt, cos, sin, dftc, p)
        ys = _layer(ys, mem_sample, cos, sin, dftc, p)
    return (yp, ys)
```

```python
import functools
import math

import numpy as np
import jax
import jax.numpy as jnp
from jax import lax
from jax.experimental import pallas as pl
from jax.experimental.pallas import tpu as pltpu

F32 = jnp.float32
BF16 = jnp.bfloat16

GRID_W = 64
F_GROUPS = 4
F_CH = 64
F_WIDTH = F_GROUPS * F_CH
HEAD_DIM = 128
N_Q_HEADS = 6
N_KV_HEADS = 2
Q_PER_KV = N_Q_HEADS // N_KV_HEADS
Q_WIDTH = N_Q_HEADS * HEAD_DIM
KV_WIDTH = N_KV_HEADS * HEAD_DIM
ROPE_AXIS_DIM = HEAD_DIM // 2
ROPE_THETA = 10000.0
X_HEADS = 4
EPS = 1e-6
LOG2E = 1.4426950408889634

VMEM_LIMIT_BYTES = 56 * 1024 * 1024
FFT_N2 = 128
SAFE_SCORE_BOUND = 60.0
SUBLANES = 8


def _params(*sem):
    return pltpu.CompilerParams(dimension_semantics=sem, vmem_limit_bytes=VMEM_LIMIT_BYTES)


def _const_spec(shape):
    nd = len(shape)
    return pl.BlockSpec(shape, lambda *_: (0,) * nd, pipeline_mode=pl.Buffered(1))


def _rms(x, g):
    ms = jnp.mean(x * x, axis=-1, keepdims=True)
    return x * lax.rsqrt(ms + EPS) * g


def _rope_tables(seq):
    rows = seq // GRID_W
    inv_freq = 1.0 / (ROPE_THETA ** (jnp.arange(0, ROPE_AXIS_DIM, 2, dtype=F32) / ROPE_AXIS_DIM))
    ang_r = jnp.arange(rows, dtype=F32)[:, None] * inv_freq[None, :]
    ang_c = jnp.arange(GRID_W, dtype=F32)[:, None] * inv_freq[None, :]

    def expand_r(a):
        return jnp.repeat(a, GRID_W, axis=0)

    def expand_c(a):
        return jnp.tile(a, (rows, 1))

    cr, sr = expand_r(jnp.cos(ang_r)), expand_r(jnp.sin(ang_r))
    cc, sc = expand_c(jnp.cos(ang_c)), expand_c(jnp.sin(ang_c))
    cos = jnp.concatenate([cr, cr, cc, cc], axis=-1)
    sin = jnp.concatenate([-sr, sr, -sc, sc], axis=-1)
    return cos, sin


def _channel_dft_table():
    c = np.arange(F_CH)
    ang = 2.0 * np.pi * np.outer(c, c) / F_CH
    cs, sn = np.cos(ang) / math.sqrt(F_CH), np.sin(ang) / math.sqrt(F_CH)
    t = np.zeros((F_WIDTH, 2 * F_WIDTH), np.float64)
    for g in range(F_GROUPS):
        sl = slice(g * F_CH, (g + 1) * F_CH)
        t[sl, sl] = cs
        t[sl, F_WIDTH + g * F_CH:F_WIDTH + (g + 1) * F_CH] = -sn
    return jnp.asarray(t, dtype=F32).astype(BF16)


def _stage1_table(n1):
    k = np.arange(n1)
    ang = 2.0 * np.pi * np.outer(k, k) / n1
    cs, sn = np.cos(ang) / math.sqrt(n1), np.sin(ang) / math.sqrt(n1)
    return jnp.asarray(np.block([[cs, sn], [-sn, cs]]), dtype=F32).astype(BF16)


def _stage2_tables(n1):
    n2 = FFT_N2
    n = n1 * n2
    alpha = 2.0 * np.pi * np.outer(np.arange(n1), np.arange(n2)) / n
    beta = 2.0 * np.pi * np.outer(np.arange(n2), np.arange(n2)) / n2
    s = 1.0 / math.sqrt(n2)
    as_f32 = lambda a: jnp.asarray(a, dtype=F32)
    return as_f32(np.cos(alpha)), as_f32(np.sin(alpha)), as_f32(np.cos(beta) * s), as_f32(np.sin(beta) * s)


def _in_proj_kernel(x_ref, gmix_ref, win_ref, dftc_ref, gq_ref, gk_ref, cos_ref, sin_ref,
                    ur_ref, ui_ref, qt_ref, k_ref, vt_ref, *, q_scale, row_splits):
    tm = x_ref.shape[0]
    sub = tm // row_splits
    lane = lax.broadcasted_iota(jnp.int32, (sub, HEAD_DIM), 1)
    first_half = (lane % ROPE_AXIS_DIM) < (ROPE_AXIS_DIM // 2)
    gq = gq_ref[...]
    gk = gk_ref[...]

    for r in range(row_splits):
        rows = slice(r * sub, (r + 1) * sub)
        h = _rms(x_ref[rows, :], gmix_ref[...]).astype(BF16)
        z = jnp.dot(h, win_ref[...], preferred_element_type=F32)

        u = jnp.dot(z[:, :F_WIDTH].astype(BF16), dftc_ref[...], preferred_element_type=F32)
        ur_ref[rows, :] = u[:, :F_WIDTH].astype(BF16)
        ui_ref[rows, :] = u[:, F_WIDTH:].astype(BF16)

        cos = cos_ref[rows, :]
        sin = sin_ref[rows, :]

        def norm_rope(zh, g):
            y = _rms(zh, g)
            partner = jnp.where(first_half,
                                pltpu.roll(y, HEAD_DIM - ROPE_AXIS_DIM // 2, 1),
                                pltpu.roll(y, ROPE_AXIS_DIM // 2, 1))
            return y * cos + partner * sin

        for hh in range(N_Q_HEADS):
            lo = F_WIDTH + hh * HEAD_DIM
            q = norm_rope(z[:, lo:lo + HEAD_DIM], gq) * q_scale
            qt_ref[hh * HEAD_DIM:(hh + 1) * HEAD_DIM, rows] = q.T.astype(BF16)
        for hh in range(N_KV_HEADS):
            lo = F_WIDTH + Q_WIDTH + hh * HEAD_DIM
            k_ref[rows, hh * HEAD_DIM:(hh + 1) * HEAD_DIM] = norm_rope(z[:, lo:lo + HEAD_DIM], gk).astype(BF16)
            lo = F_WIDTH + Q_WIDTH + KV_WIDTH + hh * HEAD_DIM
            vt_ref[hh * HEAD_DIM:(hh + 1) * HEAD_DIM, rows] = z[:, lo:lo + HEAD_DIM].T.astype(BF16)


def _in_proj(x2d, batch, seq, g_mix, w_in, dftc, g_q, g_k, cos, sin, tm=512):
    t, d = x2d.shape
    nb = seq // tm
    in_w = w_in.shape[1]
    row = lambda i: (i, 0)
    pos = lambda i: (i % nb, 0)
    tr = lambda i: (i // nb, 0, i % nb)
    q_scale = LOG2E / math.sqrt(HEAD_DIM)
    return pl.pallas_call(
        functools.partial(_in_proj_kernel, q_scale=q_scale, row_splits=2),
        grid=(t // tm,),
        in_specs=[
            pl.BlockSpec((tm, d), row),
            _const_spec((1, d)),
            _const_spec((d, in_w)),
            _const_spec((F_WIDTH, 2 * F_WIDTH)),
            _const_spec((1, HEAD_DIM)),
            _const_spec((1, HEAD_DIM)),
            pl.BlockSpec((tm, HEAD_DIM), pos),
            pl.BlockSpec((tm, HEAD_DIM), pos),
        ],
        out_specs=[
            pl.BlockSpec((tm, F_WIDTH), row),
            pl.BlockSpec((tm, F_WIDTH), row),
            pl.BlockSpec((None, Q_WIDTH, tm), tr),
            pl.BlockSpec((tm, KV_WIDTH), row),
            pl.BlockSpec((None, KV_WIDTH, tm), tr),
        ],
        out_shape=[
            jax.ShapeDtypeStruct((t, F_WIDTH), BF16),
            jax.ShapeDtypeStruct((t, F_WIDTH), BF16),
            jax.ShapeDtypeStruct((batch, Q_WIDTH, seq), BF16),
            jax.ShapeDtypeStruct((t, KV_WIDTH), BF16),
            jax.ShapeDtypeStruct((batch, KV_WIDTH, seq), BF16),
        ],
        compiler_params=_params("parallel"),
        name="in_proj",
    )(x2d, g_mix, w_in, dftc, g_q, g_k, cos, sin)


def _fourier1_kernel(m1_ref, ur_ref, ui_ref, yr_ref, yi_ref):
    n1 = ur_ref.shape[0]
    u = jnp.concatenate([ur_ref[...], ui_ref[...]], axis=0)
    y = jnp.dot(m1_ref[...], u, preferred_element_type=F32)
    yr_ref[...] = y[:n1].astype(BF16)
    yi_ref[...] = y[n1:].astype(BF16)


def _fourier2_kernel(ca_ref, sa_ref, cb_ref, sb_ref, wbd_ref, yr_ref, yi_ref, o_ref, *, kb):
    n2 = FFT_N2
    g = pl.program_id(1)
    cb = cb_ref[...]
    sb = sb_ref[...]
    wbd = wbd_ref[...]
    for j in range(kb):
        k1 = g * kb + j
        ca = ca_ref[pl.ds(k1, 1), :]
        sa = sa_ref[pl.ds(k1, 1), :]
        tw = jnp.concatenate([cb * ca - sb * sa, sb * ca + cb * sa], axis=1).astype(BF16)
        yy = jnp.concatenate([yr_ref[j * n2:(j + 1) * n2, :], yi_ref[j * n2:(j + 1) * n2, :]], axis=0)
        fr = jnp.dot(tw, yy, preferred_element_type=F32)
        o = jnp.dot(fr.astype(BF16), wbd, preferred_element_type=F32)
        o_ref[:, j * F_WIDTH:(j + 1) * F_WIDTH] = o.astype(BF16)


def _fourier(ur, ui, batch, seq, wbd, kb=8):
    n2 = FFT_N2
    n1 = seq // n2
    cols = n2 * F_WIDTH
    tc = min(cols, (4 * 1024 * 1024) // (2 * n1 * 4))
    m1 = _stage1_table(n1)
    ca, sa, cb, sb = _stage2_tables(n1)
    urv = ur.reshape(batch, n1, cols)
    uiv = ui.reshape(batch, n1, cols)
    blk1 = pl.BlockSpec((None, n1, tc), lambda b, c: (b, 0, c))
    yr, yi = pl.pallas_call(
        _fourier1_kernel,
        grid=(batch, cols // tc),
        in_specs=[_const_spec((2 * n1, 2 * n1)), blk1, blk1],
        out_specs=[blk1, blk1],
        out_shape=[jax.ShapeDtypeStruct((batch, n1, cols), BF16)] * 2,
        compiler_params=_params("parallel", "parallel"),
        name="fourier_stage1",
    )(m1, urv, uiv)

    kb = min(kb, n1)
    yblk = pl.BlockSpec((None, kb * n2, F_WIDTH), lambda b, g: (b, g, 0))
    out = pl.pallas_call(
        functools.partial(_fourier2_kernel, kb=kb),
        grid=(batch, n1 // kb),
        in_specs=[_const_spec((n1, n2)), _const_spec((n1, n2)), _const_spec((n2, n2)), _const_spec((n2, n2)),
                  _const_spec((F_WIDTH, F_WIDTH)), yblk, yblk],
        out_specs=pl.BlockSpec((None, n2, kb * F_WIDTH), lambda b, g: (b, 0, g)),
        out_shape=jax.ShapeDtypeStruct((batch, n2, n1 * F_WIDTH), BF16),
        compiler_params=_params("parallel", "parallel"),
        name="fourier_stage2",
    )(ca, sa, cb, sb, wbd, yr.reshape(batch, seq, F_WIDTH), yi.reshape(batch, seq, F_WIDTH))
    return out.reshape(batch * seq, F_WIDTH)


def _attention_kernel(qt_ref, k_ref, vt_ref, o_ref, m_ref, l_ref, acc_ref):
    j = pl.program_id(3)

    @pl.when(j == 0)
    def _():
        m_ref[...] = jnp.full(m_ref.shape, -jnp.inf, F32)
        l_ref[...] = jnp.zeros(l_ref.shape, F32)
        acc_ref[...] = jnp.zeros(acc_ref.shape, F32)

    k = k_ref[...]
    vt = vt_ref[...]
    for g in range(Q_PER_KV):
        qt = qt_ref[g * HEAD_DIM:(g + 1) * HEAD_DIM, :]
        s = jnp.dot(k, qt, preferred_element_type=F32)
        m_prev = m_ref[g]
        m_new = jnp.maximum(m_prev, jnp.max(s, axis=0, keepdims=True))
        alpha = jnp.exp2(m_prev - m_new)
        p = jnp.exp2(s - m_new)
        l_ref[g] = alpha * l_ref[g] + jnp.sum(p, axis=0, keepdims=True)
        acc_ref[g] = alpha * acc_ref[g] + jnp.dot(vt, p.astype(BF16), preferred_element_type=F32)
        m_ref[g] = m_new

    @pl.when(j == pl.num_programs(3) - 1)
    def _():
        for g in range(Q_PER_KV):
            o = acc_ref[g] * (1.0 / l_ref[g])
            o_ref[:, g * HEAD_DIM:(g + 1) * HEAD_DIM] = o.T.astype(BF16)


def _attention_bounded_kernel(qt_ref, k_ref, vt_ref, o_ref, l_ref, acc_ref):
    j = pl.program_id(3)

    @pl.when(j == 0)
    def _():
        l_ref[...] = jnp.zeros(l_ref.shape, F32)
        acc_ref[...] = jnp.zeros(acc_ref.shape, F32)

    k = k_ref[...]
    vt = vt_ref[...]
    tk = k.shape[0]
    for g in range(Q_PER_KV):
        s = jnp.dot(k, qt_ref[g * HEAD_DIM:(g + 1) * HEAD_DIM, :], preferred_element_type=F32)
        p = jnp.exp2(s)
        l_ref[g] += jnp.sum(p.reshape(tk // SUBLANES, SUBLANES, p.shape[1]), axis=0)
        acc_ref[g] += jnp.dot(vt, p.astype(BF16), preferred_element_type=F32)

    @pl.when(j == pl.num_programs(3) - 1)
    def _():
        for g in range(Q_PER_KV):
            denom = jnp.sum(l_ref[g], axis=0, keepdims=True)
            o = acc_ref[g] * (1.0 / denom)
            o_ref[:, g * HEAD_DIM:(g + 1) * HEAD_DIM] = o.T.astype(BF16)


def _attention_call(body, scratch, qt, k, vt, tq, tk, name):
    batch, _, seq = qt.shape
    tq, tk = min(tq, seq), min(tk, seq)
    gw = Q_PER_KV * HEAD_DIM
    return pl.pallas_call(
        body,
        grid=(batch, N_KV_HEADS, seq // tq, seq // tk),
        in_specs=[
            pl.BlockSpec((None, gw, tq), lambda b, h, i, j: (b, h, i)),
            pl.BlockSpec((None, tk, HEAD_DIM), lambda b, h, i, j: (b, j, h)),
            pl.BlockSpec((None, HEAD_DIM, tk), lambda b, h, i, j: (b, h, j)),
        ],
        out_specs=pl.BlockSpec((None, tq, gw), lambda b, h, i, j: (b, i, h)),
        out_shape=jax.ShapeDtypeStruct((batch, seq, Q_WIDTH), BF16),
        scratch_shapes=scratch(tq),
        compiler_params=_params("parallel", "parallel", "parallel", "arbitrary"),
        name=name,
    )(qt, k, vt)


def _attention(qt, k, vt, score_bound):
    online_scratch = lambda tq: [pltpu.VMEM((Q_PER_KV, 1, tq), F32), pltpu.VMEM((Q_PER_KV, 1, tq), F32),
                                 pltpu.VMEM((Q_PER_KV, HEAD_DIM, tq), F32)]
    bounded_scratch = lambda tq: [pltpu.VMEM((Q_PER_KV, SUBLANES, tq), F32),
                                  pltpu.VMEM((Q_PER_KV, HEAD_DIM, tq), F32)]
    bounded = functools.partial(_attention_call, _attention_bounded_kernel, bounded_scratch,
                                tq=1024, tk=1024, name="self_attention_bounded")
    online = functools.partial(_attention_call, _attention_kernel, online_scratch,
                               tq=512, tk=512, name="self_attention_online")
    return lax.cond(score_bound <= SAFE_SCORE_BOUND, bounded, online, qt, k, vt)


def _mem_kv_kernel(mem_ref, gmem_ref, wckv_ref, gck_ref, kt_ref, v_ref):
    d = mem_ref.shape[1]
    xh = d // X_HEADS
    h = _rms(mem_ref[...], gmem_ref[...]).astype(BF16)
    kv = jnp.dot(h, wckv_ref[...], preferred_element_type=F32)
    gck = gck_ref[...]
    for hh in range(X_HEADS):
        kh = _rms(kv[:, hh * xh:(hh + 1) * xh], gck)
        kt_ref[hh * xh:(hh + 1) * xh, :] = kh.T.astype(BF16)
    v_ref[...] = kv[:, d:].astype(BF16)


def _mem_kv(mem, g_mem, w_ckv, g_ck):
    batch, m, d = mem.shape
    return pl.pallas_call(
        _mem_kv_kernel,
        grid=(batch,),
        in_specs=[pl.BlockSpec((None, m, d), lambda b: (b, 0, 0)),
                  _const_spec((1, d)), _const_spec((d, 2 * d)), _const_spec((1, d // X_HEADS))],
        out_specs=[pl.BlockSpec((None, d, m), lambda b: (b, 0, 0)),
                   pl.BlockSpec((None, m, d), lambda b: (b, 0, 0))],
        out_shape=[jax.ShapeDtypeStruct((batch, d, m), BF16), jax.ShapeDtypeStruct((batch, m, d), BF16)],
        compiler_params=_params("parallel"),
        name="mem_kv",
    )(mem, g_mem, w_ckv, g_ck)


def _post_kernel(x_ref, f_ref, a_ref, wout_ref, gcross_ref, wcq_ref, gcq_ref, kt_ref, v_ref, wco_ref,
                 gmlp_ref, wup_ref, wdown_ref, y_ref, *, x_scale, ff_chunk):
    d = x_ref.shape[1]
    xh = d // X_HEADS
    x1 = (x_ref[...]
          + jnp.dot(f_ref[...], wout_ref[:F_WIDTH, :], preferred_element_type=F32)
          + jnp.dot(a_ref[...], wout_ref[F_WIDTH:, :], preferred_element_type=F32))

    qc = jnp.dot(_rms(x1, gcross_ref[...]).astype(BF16), wcq_ref[...], preferred_element_type=F32)
    gcq = gcq_ref[...]
    heads = []
    for hh in range(X_HEADS):
        sl = slice(hh * xh, (hh + 1) * xh)
        qh = (_rms(qc[:, sl], gcq) * x_scale).astype(BF16)
        s = jnp.dot(qh, kt_ref[sl, :], preferred_element_type=F32)
        p = jnp.exp2(s - jnp.max(s, axis=-1, keepdims=True))
        o = jnp.dot(p.astype(BF16), v_ref[:, sl], preferred_element_type=F32)
        heads.append((o * (1.0 / jnp.sum(p, axis=-1, keepdims=True))).astype(BF16))
    x2 = x1 + jnp.dot(jnp.concatenate(heads, axis=-1), wco_ref[...], preferred_element_type=F32)

    h3 = _rms(x2, gmlp_ref[...]).astype(BF16)
    acc = x2
    for c in range(wup_ref.shape[1] // ff_chunk):
        sl = slice(c * ff_chunk, (c + 1) * ff_chunk)
        u = jnp.maximum(jnp.dot(h3, wup_ref[:, sl], preferred_element_type=F32), 0.0)
        acc = acc + jnp.dot((u * u).astype(BF16), wdown_ref[sl, :], preferred_element_type=F32)
    y_ref[...] = acc


def _post(x2d, f_out, a_out, kt, v, seq, w_out, g_cross, w_cq, g_cq, w_co, g_mlp, w_up, w_down, tm=512):
    t, d = x2d.shape
    nb = seq // tm
    m = v.shape[1]
    d_ff = w_up.shape[1]
    row = lambda i: (i, 0)
    x_scale = LOG2E / math.sqrt(d // X_HEADS)
    return pl.pallas_call(
        functools.partial(_post_kernel, x_scale=x_scale, ff_chunk=1024),
        grid=(t // tm,),
        in_specs=[
            pl.BlockSpec((tm, d), row),
            pl.BlockSpec((tm, F_WIDTH), row),
            pl.BlockSpec((tm, Q_WIDTH), row),
            _const_spec((d, d)),
            _const_spec((1, d)),
            _const_spec((d, d)),
            _const_spec((1, d // X_HEADS)),
            pl.BlockSpec((None, d, m), lambda i: (i // nb, 0, 0)),
            pl.BlockSpec((None, m, d), lambda i: (i // nb, 0, 0)),
            _const_spec((d, d)),
            _const_spec((1, d)),
            _const_spec((d, d_ff)),
            _const_spec((d_ff, d)),
        ],
        out_specs=pl.BlockSpec((tm, d), row),
        out_shape=jax.ShapeDtypeStruct((t, d), F32),
        compiler_params=_params("parallel"),
        name="post",
    )(x2d, f_out, a_out, w_out, g_cross, w_cq, g_cq, kt, v, w_co, g_mlp, w_up, w_down)


def _block_diag_weights(w_f):
    out = jnp.zeros((F_WIDTH, F_WIDTH), w_f.dtype)
    for g in range(F_GROUPS):
        out = out.at[g * F_CH:(g + 1) * F_CH, g * F_CH:(g + 1) * F_CH].set(w_f[g])
    return out


def _layer(x, mem, cos, sin, dftc, p):
    batch, seq, d = x.shape
    x2d = x.reshape(batch * seq, d)
    ur, ui, qt, k, vt = _in_proj(x2d, batch, seq, p["g_mix"], p["w_in"], dftc, p["g_q"], p["g_k"],
                                 cos[:seq], sin[:seq])
    f_out = _fourier(ur, ui, batch, seq, p["wbd"])
    score_bound = (HEAD_DIM * LOG2E / math.sqrt(HEAD_DIM)) * jnp.max(jnp.abs(p["g_q"])) * jnp.max(jnp.abs(p["g_k"]))
    a_out = _attention(qt, k.reshape(batch, seq, KV_WIDTH), vt, score_bound).reshape(batch * seq, Q_WIDTH)
    kt, v = _mem_kv(mem, p["g_mem"], p["w_ckv"], p["g_ck"])
    y = _post(x2d, f_out, a_out, kt, v, seq, p["w_out"], p["g_cross"], p["w_cq"], p["g_cq"], p["w_co"],
              p["g_mlp"], p["w_up"], p["w_down"])
    return y.reshape(batch, seq, d)


def kernel(x_prompt, x_sample, mem_prompt, mem_sample, g_mix, w_in, w_fourier, g_q, g_k, w_out, g_cross, g_mem,
           w_cq, w_ckv, g_cq, g_ck, w_co, g_mlp, w_up, w_down):
    depth = w_in.shape[0]
    max_seq = max(x_prompt.shape[1], x_sample.shape[1])
    cos, sin = _rope_tables(max_seq)
    dftc = _channel_dft_table()
    yp, ys = x_prompt, x_sample
    for l in range(depth):
        row = lambda g: g[l][None, :]
        p = dict(
            g_mix=row(g_mix), w_in=w_in[l].astype(BF16), wbd=_block_diag_weights(w_fourier[l]).astype(BF16),
            g_q=row(g_q), g_k=row(g_k), w_out=w_out[l].astype(BF16), g_cross=row(g_cross), g_mem=row(g_mem),
            w_cq=w_cq[l].astype(BF16), w_ckv=w_ckv[l].astype(BF16), g_cq=row(g_cq), g_ck=row(g_ck),
            w_co=w_co[l].astype(BF16), g_mlp=row(g_mlp), w_up=w_up[l].astype(BF16), w_down=w_down[l].astype(BF16),
        )
        yp = _layer(yp, mem_prompt, cos, sin, dftc, p)
        ys = _layer(ys, mem_sample, cos, sin, dftc, p)
    return (yp, ys)
```

```python
import functools
import math

import numpy as np
import jax
import jax.numpy as jnp
from jax import lax
from jax.experimental import pallas as pl
from jax.experimental.pallas import tpu as pltpu

F32 = jnp.float32
BF16 = jnp.bfloat16

GRID_W = 64
F_GROUPS = 4
F_CH = 64
F_WIDTH = F_GROUPS * F_CH
HEAD_DIM = 128
N_Q_HEADS = 6
N_KV_HEADS = 2
Q_PER_KV = N_Q_HEADS // N_KV_HEADS
Q_WIDTH = N_Q_HEADS * HEAD_DIM
KV_WIDTH = N_KV_HEADS * HEAD_DIM
ROPE_AXIS_DIM = HEAD_DIM // 2
ROPE_THETA = 10000.0
X_HEADS = 4
EPS = 1e-6
LOG2E = 1.4426950408889634

VMEM_LIMIT_BYTES = 56 * 1024 * 1024
FFT_N2 = 128
SAFE_SCORE_BOUND = 60.0
SUBLANES = 8


def _params(*sem):
    return pltpu.CompilerParams(dimension_semantics=sem, vmem_limit_bytes=VMEM_LIMIT_BYTES)


def _const_spec(shape):
    nd = len(shape)
    return pl.BlockSpec(shape, lambda *_: (0,) * nd, pipeline_mode=pl.Buffered(1))


def _rms(x, g):
    ms = jnp.mean(x * x, axis=-1, keepdims=True)
    return x * lax.rsqrt(ms + EPS) * g


def _rope_tables(seq):
    rows = seq // GRID_W
    inv_freq = 1.0 / (ROPE_THETA ** (jnp.arange(0, ROPE_AXIS_DIM, 2, dtype=F32) / ROPE_AXIS_DIM))
    ang_r = jnp.arange(rows, dtype=F32)[:, None] * inv_freq[None, :]
    ang_c = jnp.arange(GRID_W, dtype=F32)[:, None] * inv_freq[None, :]

    def expand_r(a):
        return jnp.repeat(a, GRID_W, axis=0)

    def expand_c(a):
        return jnp.tile(a, (rows, 1))

    cr, sr = expand_r(jnp.cos(ang_r)), expand_r(jnp.sin(ang_r))
    cc, sc = expand_c(jnp.cos(ang_c)), expand_c(jnp.sin(ang_c))
    cos = jnp.concatenate([cr, cr, cc, cc], axis=-1)
    sin = jnp.concatenate([-sr, sr, -sc, sc], axis=-1)
    return cos, sin


def _channel_dft_table():
    c = np.arange(F_CH)
    ang = 2.0 * np.pi * np.outer(c, c) / F_CH
    cs, sn = np.cos(ang) / math.sqrt(F_CH), np.sin(ang) / math.sqrt(F_CH)
    t = np.zeros((F_WIDTH, 2 * F_WIDTH), np.float64)
    for g in range(F_GROUPS):
        sl = slice(g * F_CH, (g + 1) * F_CH)
        t[sl, sl] = cs
        t[sl, F_WIDTH + g * F_CH:F_WIDTH + (g + 1) * F_CH] = -sn
    return jnp.asarray(t, dtype=F32).astype(BF16)


def _stage1_table(n1):
    k = np.arange(n1)
    ang = 2.0 * np.pi * np.outer(k, k) / n1
    cs, sn = np.cos(ang) / math.sqrt(n1), np.sin(ang) / math.sqrt(n1)
    return jnp.asarray(np.block([[cs, sn], [-sn, cs]]), dtype=F32).astype(BF16)


def _stage2_tables(n1):
    n2 = FFT_N2
    n = n1 * n2
    alpha = 2.0 * np.pi * np.outer(np.arange(n1), np.arange(n2)) / n
    beta = 2.0 * np.pi * np.outer(np.arange(n2), np.arange(n2)) / n2
    s = 1.0 / math.sqrt(n2)
    as_f32 = lambda a: jnp.asarray(a, dtype=F32)
    return as_f32(np.cos(alpha)), as_f32(np.sin(alpha)), as_f32(np.cos(beta) * s), as_f32(np.sin(beta) * s)


def _in_proj_kernel(x_ref, gmix_ref, win_ref, dftc_ref, gq_ref, gk_ref, cos_ref, sin_ref,
                    ur_ref, ui_ref, qt_ref, k_ref, vt_ref, *, q_scale, row_splits):
    tm = x_ref.shape[0]
    sub = tm // row_splits
    lane = lax.broadcasted_iota(jnp.int32, (sub, HEAD_DIM), 1)
    first_half = (lane % ROPE_AXIS_DIM) < (ROPE_AXIS_DIM // 2)
    gq = gq_ref[...] * q_scale
    gk = gk_ref[...]

    for r in range(row_splits):
        rows = slice(r * sub, (r + 1) * sub)
        h = _rms(x_ref[rows, :], gmix_ref[...]).astype(BF16)
        z = jnp.dot(h, win_ref[...], preferred_element_type=F32)

        u = jnp.dot(z[:, :F_WIDTH].astype(BF16), dftc_ref[...], preferred_element_type=F32)
        ur_ref[rows, :] = u[:, :F_WIDTH].astype(BF16)
        ui_ref[rows, :] = u[:, F_WIDTH:].astype(BF16)

        cos = cos_ref[rows, :]
        sin = sin_ref[rows, :]

        def norm_rope(zh, g):
            y = _rms(zh, g)
            partner = jnp.where(first_half,
                                pltpu.roll(y, HEAD_DIM - ROPE_AXIS_DIM // 2, 1),
                                pltpu.roll(y, ROPE_AXIS_DIM // 2, 1))
            return y * cos + partner * sin

        for hh in range(N_Q_HEADS):
            lo = F_WIDTH + hh * HEAD_DIM
            q = norm_rope(z[:, lo:lo + HEAD_DIM], gq)
            qt_ref[hh * HEAD_DIM:(hh + 1) * HEAD_DIM, rows] = q.T.astype(BF16)
        for hh in range(N_KV_HEADS):
            lo = F_WIDTH + Q_WIDTH + hh * HEAD_DIM
            k_ref[rows, hh * HEAD_DIM:(hh + 1) * HEAD_DIM] = norm_rope(z[:, lo:lo + HEAD_DIM], gk).astype(BF16)
            lo = F_WIDTH + Q_WIDTH + KV_WIDTH + hh * HEAD_DIM
            vt_ref[hh * HEAD_DIM:(hh + 1) * HEAD_DIM, rows] = z[:, lo:lo + HEAD_DIM].T.astype(BF16)


def _in_proj(x2d, batch, seq, g_mix, w_in, dftc, g_q, g_k, cos, sin, tm=512):
    t, d = x2d.shape
    nb = seq // tm
    in_w = w_in.shape[1]
    row = lambda i: (i, 0)
    pos = lambda i: (i % nb, 0)
    tr = lambda i: (i // nb, 0, i % nb)
    q_scale = LOG2E / math.sqrt(HEAD_DIM)
    return pl.pallas_call(
        functools.partial(_in_proj_kernel, q_scale=q_scale, row_splits=2),
        grid=(t // tm,),
        in_specs=[
            pl.BlockSpec((tm, d), row),
            _const_spec((1, d)),
            _const_spec((d, in_w)),
            _const_spec((F_WIDTH, 2 * F_WIDTH)),
            _const_spec((1, HEAD_DIM)),
            _const_spec((1, HEAD_DIM)),
            pl.BlockSpec((tm, HEAD_DIM), pos),
            pl.BlockSpec((tm, HEAD_DIM), pos),
        ],
        out_specs=[
            pl.BlockSpec((tm, F_WIDTH), row),
            pl.BlockSpec((tm, F_WIDTH), row),
            pl.BlockSpec((None, Q_WIDTH, tm), tr),
            pl.BlockSpec((tm, KV_WIDTH), row),
            pl.BlockSpec((None, KV_WIDTH, tm), tr),
        ],
        out_shape=[
            jax.ShapeDtypeStruct((t, F_WIDTH), BF16),
            jax.ShapeDtypeStruct((t, F_WIDTH), BF16),
            jax.ShapeDtypeStruct((batch, Q_WIDTH, seq), BF16),
            jax.ShapeDtypeStruct((t, KV_WIDTH), BF16),
            jax.ShapeDtypeStruct((batch, KV_WIDTH, seq), BF16),
        ],
        compiler_params=_params("parallel"),
        name="in_proj",
    )(x2d, g_mix, w_in, dftc, g_q, g_k, cos, sin)


def _fourier1_kernel(m1_ref, ur_ref, ui_ref, yr_ref, yi_ref):
    n1 = ur_ref.shape[0]
    u = jnp.concatenate([ur_ref[...], ui_ref[...]], axis=0)
    y = jnp.dot(m1_ref[...], u, preferred_element_type=F32)
    yr_ref[...] = y[:n1].astype(BF16)
    yi_ref[...] = y[n1:].astype(BF16)


def _fourier2_kernel(ca_ref, sa_ref, cb_ref, sb_ref, wbd_ref, yr_ref, yi_ref, o_ref, *, kb):
    n2 = FFT_N2
    g = pl.program_id(1)
    cb = cb_ref[...]
    sb = sb_ref[...]
    wbd = wbd_ref[...]
    for j in range(kb):
        k1 = g * kb + j
        ca = ca_ref[pl.ds(k1, 1), :]
        sa = sa_ref[pl.ds(k1, 1), :]
        tw = jnp.concatenate([cb * ca - sb * sa, sb * ca + cb * sa], axis=1).astype(BF16)
        yy = jnp.concatenate([yr_ref[j * n2:(j + 1) * n2, :], yi_ref[j * n2:(j + 1) * n2, :]], axis=0)
        fr = jnp.dot(tw, yy, preferred_element_type=F32)
        o = jnp.dot(fr.astype(BF16), wbd, preferred_element_type=F32)
        o_ref[:, j * F_WIDTH:(j + 1) * F_WIDTH] = o.astype(BF16)


def _fourier(ur, ui, batch, seq, wbd, kb=8):
    n2 = FFT_N2
    n1 = seq // n2
    cols = n2 * F_WIDTH
    tc = min(cols, (4 * 1024 * 1024) // (2 * n1 * 4))
    m1 = _stage1_table(n1)
    ca, sa, cb, sb = _stage2_tables(n1)
    urv = ur.reshape(batch, n1, cols)
    uiv = ui.reshape(batch, n1, cols)
    blk1 = pl.BlockSpec((None, n1, tc), lambda b, c: (b, 0, c))
    yr, yi = pl.pallas_call(
        _fourier1_kernel,
        grid=(batch, cols // tc),
        in_specs=[_const_spec((2 * n1, 2 * n1)), blk1, blk1],
        out_specs=[blk1, blk1],
        out_shape=[jax.ShapeDtypeStruct((batch, n1, cols), BF16)] * 2,
        compiler_params=_params("parallel", "parallel"),
        name="fourier_stage1",
    )(m1, urv, uiv)

    kb = min(kb, n1)
    yblk = pl.BlockSpec((None, kb * n2, F_WIDTH), lambda b, g: (b, g, 0))
    out = pl.pallas_call(
        functools.partial(_fourier2_kernel, kb=kb),
        grid=(batch, n1 // kb),
        in_specs=[_const_spec((n1, n2)), _const_spec((n1, n2)), _const_spec((n2, n2)), _const_spec((n2, n2)),
                  _const_spec((F_WIDTH, F_WIDTH)), yblk, yblk],
        out_specs=pl.BlockSpec((None, n2, kb * F_WIDTH), lambda b, g: (b, 0, g)),
        out_shape=jax.ShapeDtypeStruct((batch, n2, n1 * F_WIDTH), BF16),
        compiler_params=_params("parallel", "parallel"),
        name="fourier_stage2",
    )(ca, sa, cb, sb, wbd, yr.reshape(batch, seq, F_WIDTH), yi.reshape(batch, seq, F_WIDTH))
    return out.reshape(batch * seq, F_WIDTH)


def _attention_kernel(qt_ref, k_ref, vt_ref, o_ref, m_ref, l_ref, acc_ref):
    j = pl.program_id(3)

    @pl.when(j == 0)
    def _():
        m_ref[...] = jnp.full(m_ref.shape, -jnp.inf, F32)
        l_ref[...] = jnp.zeros(l_ref.shape, F32)
        acc_ref[...] = jnp.zeros(acc_ref.shape, F32)

    k = k_ref[...]
    vt = vt_ref[...]
    for g in range(Q_PER_KV):
        qt = qt_ref[g * HEAD_DIM:(g + 1) * HEAD_DIM, :]
        s = jnp.dot(k, qt, preferred_element_type=F32)
        m_prev = m_ref[g]
        m_new = jnp.maximum(m_prev, jnp.max(s, axis=0, keepdims=True))
        alpha = jnp.exp2(m_prev - m_new)
        p = jnp.exp2(s - m_new)
        l_ref[g] = alpha * l_ref[g] + jnp.sum(p, axis=0, keepdims=True)
        acc_ref[g] = alpha * acc_ref[g] + jnp.dot(vt, p.astype(BF16), preferred_element_type=F32)
        m_ref[g] = m_new

    @pl.when(j == pl.num_programs(3) - 1)
    def _():
        for g in range(Q_PER_KV):
            o = acc_ref[g] * (1.0 / l_ref[g])
            o_ref[:, g * HEAD_DIM:(g + 1) * HEAD_DIM] = o.T.astype(BF16)


def _attention_bounded_kernel(qt_ref, k_ref, vt_ref, o_ref, l_ref, acc_ref):
    j = pl.program_id(3)

    @pl.when(j == 0)
    def _():
        l_ref[...] = jnp.zeros(l_ref.shape, F32)
        acc_ref[...] = jnp.zeros(acc_ref.shape, F32)

    k = k_ref[...]
    vt = vt_ref[...]
    tk = k.shape[0]
    for g in range(Q_PER_KV):
        s = jnp.dot(k, qt_ref[g * HEAD_DIM:(g + 1) * HEAD_DIM, :], preferred_element_type=F32)
        p = jnp.exp2(s)
        l_ref[g] += jnp.sum(p.reshape(tk // SUBLANES, SUBLANES, p.shape[1]), axis=0)
        acc_ref[g] += jnp.dot(vt, p.astype(BF16), preferred_element_type=F32)

    @pl.when(j == pl.num_programs(3) - 1)
    def _():
        for g in range(Q_PER_KV):
            denom = jnp.sum(l_ref[g], axis=0, keepdims=True)
            o = acc_ref[g] * (1.0 / denom)
            o_ref[:, g * HEAD_DIM:(g + 1) * HEAD_DIM] = o.T.astype(BF16)


def _attention_call(body, scratch, qt, k, vt, tq, tk, name):
    batch, _, seq = qt.shape
    tq, tk = min(tq, seq), min(tk, seq)
    gw = Q_PER_KV * HEAD_DIM
    return pl.pallas_call(
        body,
        grid=(batch, N_KV_HEADS, seq // tq, seq // tk),
        in_specs=[
            pl.BlockSpec((None, gw, tq), lambda b, h, i, j: (b, h, i)),
            pl.BlockSpec((None, tk, HEAD_DIM), lambda b, h, i, j: (b, j, h)),
            pl.BlockSpec((None, HEAD_DIM, tk), lambda b, h, i, j: (b, h, j)),
        ],
        out_specs=pl.BlockSpec((None, tq, gw), lambda b, h, i, j: (b, i, h)),
        out_shape=jax.ShapeDtypeStruct((batch, seq, Q_WIDTH), BF16),
        scratch_shapes=scratch(tq),
        compiler_params=_params("parallel", "parallel", "parallel", "arbitrary"),
        name=name,
    )(qt, k, vt)


def _attention(qt, k, vt, score_bound):
    online_scratch = lambda tq: [pltpu.VMEM((Q_PER_KV, 1, tq), F32), pltpu.VMEM((Q_PER_KV, 1, tq), F32),
                                 pltpu.VMEM((Q_PER_KV, HEAD_DIM, tq), F32)]
    bounded_scratch = lambda tq: [pltpu.VMEM((Q_PER_KV, SUBLANES, tq), F32),
                                  pltpu.VMEM((Q_PER_KV, HEAD_DIM, tq), F32)]
    bounded = functools.partial(_attention_call, _attention_bounded_kernel, bounded_scratch,
                                tq=512, tk=8192, name="self_attention_bounded")
    online = functools.partial(_attention_call, _attention_kernel, online_scratch,
                               tq=512, tk=512, name="self_attention_online")
    return lax.cond(score_bound <= SAFE_SCORE_BOUND, bounded, online, qt, k, vt)


def _mem_kv_kernel(mem_ref, gmem_ref, wckv_ref, gck_ref, kt_ref, v_ref):
    d = mem_ref.shape[1]
    xh = d // X_HEADS
    h = _rms(mem_ref[...], gmem_ref[...]).astype(BF16)
    kv = jnp.dot(h, wckv_ref[...], preferred_element_type=F32)
    gck = gck_ref[...]
    for hh in range(X_HEADS):
        kh = _rms(kv[:, hh * xh:(hh + 1) * xh], gck)
        kt_ref[hh * xh:(hh + 1) * xh, :] = kh.T.astype(BF16)
    v_ref[...] = kv[:, d:].astype(BF16)


def _mem_kv(mem, g_mem, w_ckv, g_ck):
    batch, m, d = mem.shape
    return pl.pallas_call(
        _mem_kv_kernel,
        grid=(batch,),
        in_specs=[pl.BlockSpec((None, m, d), lambda b: (b, 0, 0)),
                  _const_spec((1, d)), _const_spec((d, 2 * d)), _const_spec((1, d // X_HEADS))],
        out_specs=[pl.BlockSpec((None, d, m), lambda b: (b, 0, 0)),
                   pl.BlockSpec((None, m, d), lambda b: (b, 0, 0))],
        out_shape=[jax.ShapeDtypeStruct((batch, d, m), BF16), jax.ShapeDtypeStruct((batch, m, d), BF16)],
        compiler_params=_params("parallel"),
        name="mem_kv",
    )(mem, g_mem, w_ckv, g_ck)


def _post_kernel(x_ref, f_ref, a_ref, wout_ref, gcross_ref, wcq_ref, gcq_ref, kt_ref, v_ref, wco_ref,
                 gmlp_ref, wup_ref, wdown_ref, y_ref, *, x_scale, ff_chunk):
    d = x_ref.shape[1]
    xh = d // X_HEADS
    x1 = (x_ref[...]
          + jnp.dot(f_ref[...], wout_ref[:F_WIDTH, :], preferred_element_type=F32)
          + jnp.dot(a_ref[...], wout_ref[F_WIDTH:, :], preferred_element_type=F32))

    qc = jnp.dot(_rms(x1, gcross_ref[...]).astype(BF16), wcq_ref[...], preferred_element_type=F32)
    gcq = gcq_ref[...]
    heads = []
    for hh in range(X_HEADS):
        sl = slice(hh * xh, (hh + 1) * xh)
        qh = (_rms(qc[:, sl], gcq) * x_scale).astype(BF16)
        s = jnp.dot(qh, kt_ref[sl, :], preferred_element_type=F32)
        p = jnp.exp2(s - jnp.max(s, axis=-1, keepdims=True))
        o = jnp.dot(p.astype(BF16), v_ref[:, sl], preferred_element_type=F32)
        heads.append((o * (1.0 / jnp.sum(p, axis=-1, keepdims=True))).astype(BF16))
    x2 = x1 + jnp.dot(jnp.concatenate(heads, axis=-1), wco_ref[...], preferred_element_type=F32)

    h3 = _rms(x2, gmlp_ref[...]).astype(BF16)
    acc = x2
    for c in range(wup_ref.shape[1] // ff_chunk):
        sl = slice(c * ff_chunk, (c + 1) * ff_chunk)
        u = jnp.maximum(jnp.dot(h3, wup_ref[:, sl], preferred_element_type=F32), 0.0)
        acc = acc + jnp.dot((u * u).astype(BF16), wdown_ref[sl, :], preferred_element_type=F32)
    y_ref[...] = acc


def _post(x2d, f_out, a_out, kt, v, seq, w_out, g_cross, w_cq, g_cq, w_co, g_mlp, w_up, w_down, tm=512):
    t, d = x2d.shape
    nb = seq // tm
    m = v.shape[1]
    d_ff = w_up.shape[1]
    row = lambda i: (i, 0)
    x_scale = LOG2E / math.sqrt(d // X_HEADS)
    return pl.pallas_call(
        functools.partial(_post_kernel, x_scale=x_scale, ff_chunk=1024),
        grid=(t // tm,),
        in_specs=[
            pl.BlockSpec((tm, d), row),
            pl.BlockSpec((tm, F_WIDTH), row),
            pl.BlockSpec((tm, Q_WIDTH), row),
            _const_spec((d, d)),
            _const_spec((1, d)),
            _const_spec((d, d)),
            _const_spec((1, d // X_HEADS)),
            pl.BlockSpec((None, d, m), lambda i: (i // nb, 0, 0)),
            pl.BlockSpec((None, m, d), lambda i: (i // nb, 0, 0)),
            _const_spec((d, d)),
            _const_spec((1, d)),
            _const_spec((d, d_ff)),
            _const_spec((d_ff, d)),
        ],
        out_specs=pl.BlockSpec((tm, d), row),
        out_shape=jax.ShapeDtypeStruct((t, d), F32),
        compiler_params=_params("parallel"),
        name="post",
    )(x2d, f_out, a_out, w_out, g_cross, w_cq, g_cq, kt, v, w_co, g_mlp, w_up, w_down)


def _block_diag_weights(w_f):
    out = jnp.zeros((F_WIDTH, F_WIDTH), w_f.dtype)
    for g in range(F_GROUPS):
        out = out.at[g * F_CH:(g + 1) * F_CH, g * F_CH:(g + 1) * F_CH].set(w_f[g])
    return out


def _layer(x, mem, cos, sin, dftc, p):
    batch, seq, d = x.shape
    x2d = x.reshape(batch * seq, d)
    ur, ui, qt, k, vt = _in_proj(x2d, batch, seq, p["g_mix"], p["w_in"], dftc, p["g_q"], p["g_k"],
                                 cos[:seq], sin[:seq])
    f_out = _fourier(ur, ui, batch, seq, p["wbd"])
    score_bound = (HEAD_DIM * LOG2E / math.sqrt(HEAD_DIM)) * jnp.max(jnp.abs(p["g_q"])) * jnp.max(jnp.abs(p["g_k"]))
    a_out = _attention(qt, k.reshape(batch, seq, KV_WIDTH), vt, score_bound).reshape(batch * seq, Q_WIDTH)
    kt, v = _mem_kv(mem, p["g_mem"], p["w_ckv"], p["g_ck"])
    y = _post(x2d, f_out, a_out, kt, v, seq, p["w_out"], p["g_cross"], p["w_cq"], p["g_cq"], p["w_co"],
              p["g_mlp"], p["w_up"], p["w_down"])
    return y.reshape(batch, seq, d)


def kernel(x_prompt, x_sample, mem_prompt, mem_sample, g_mix, w_in, w_fourier, g_q, g_k, w_out, g_cross, g_mem,
           w_cq, w_ckv, g_cq, g_ck, w_co, g_mlp, w_up, w_down):
    depth = w_in.shape[0]
    max_seq = max(x_prompt.shape[1], x_sample.shape[1])
    cos, sin = _rope_tables(max_seq)
    dftc = _channel_dft_table()
    yp, ys = x_prompt, x_sample
    for l in range(depth):
        row = lambda g: g[l][None, :]
        p = dict(
            g_mix=row(g_mix), w_in=w_in[l].astype(BF16), wbd=_block_diag_weights(w_fourier[l]).astype(BF16),
            g_q=row(g_q), g_k=row(g_k), w_out=w_out[l].astype(BF16), g_cross=row(g_cross), g_mem=row(g_mem),
            w_cq=w_cq[l].astype(BF16), w_ckv=w_ckv[l].astype(BF16), g_cq=row(g_cq), g_ck=row(g_ck),
            w_co=w_co[l].astype(BF16), g_mlp=row(g_mlp), w_up=w_up[l].astype(BF16), w_down=w_down[l].astype(BF16),
        )
        yp = _layer(yp, mem_prompt, cos, sin, dftc, p)
        ys = _layer(ys, mem_sample, cos, sin, dftc, p)
    return (yp, ys)
```

```python
import functools
import math

import numpy as np
import jax
import jax.numpy as jnp
from jax import lax
from jax.experimental import pallas as pl
from jax.experimental.pallas import tpu as pltpu

F32 = jnp.float32
BF16 = jnp.bfloat16

GRID_W = 64
F_GROUPS = 4
F_CH = 64
F_WIDTH = F_GROUPS * F_CH
HEAD_DIM = 128
N_Q_HEADS = 6
N_KV_HEADS = 2
Q_PER_KV = N_Q_HEADS // N_KV_HEADS
Q_WIDTH = N_Q_HEADS * HEAD_DIM
KV_WIDTH = N_KV_HEADS * HEAD_DIM
ROPE_AXIS_DIM = HEAD_DIM // 2
ROPE_THETA = 10000.0
X_HEADS = 4
EPS = 1e-6
LOG2E = 1.4426950408889634

VMEM_LIMIT_BYTES = 56 * 1024 * 1024
FFT_N2 = 128
SAFE_SCORE_BOUND = 60.0
SUBLANES = 8


def _params(*sem):
    return pltpu.CompilerParams(dimension_semantics=sem, vmem_limit_bytes=VMEM_LIMIT_BYTES)


def _const_spec(shape):
    nd = len(shape)
    return pl.BlockSpec(shape, lambda *_: (0,) * nd, pipeline_mode=pl.Buffered(1))


def _rms(x, g):
    ms = jnp.mean(x * x, axis=-1, keepdims=True)
    return x * lax.rsqrt(ms + EPS) * g


def _rope_tables(seq):
    rows = seq // GRID_W
    inv_freq = 1.0 / (ROPE_THETA ** (jnp.arange(0, ROPE_AXIS_DIM, 2, dtype=F32) / ROPE_AXIS_DIM))
    ang_r = jnp.arange(rows, dtype=F32)[:, None] * inv_freq[None, :]
    ang_c = jnp.arange(GRID_W, dtype=F32)[:, None] * inv_freq[None, :]

    def expand_r(a):
        return jnp.repeat(a, GRID_W, axis=0)

    def expand_c(a):
        return jnp.tile(a, (rows, 1))

    cr, sr = expand_r(jnp.cos(ang_r)), expand_r(jnp.sin(ang_r))
    cc, sc = expand_c(jnp.cos(ang_c)), expand_c(jnp.sin(ang_c))
    cos = jnp.concatenate([cr, cr, cc, cc], axis=-1)
    sin = jnp.concatenate([-sr, sr, -sc, sc], axis=-1)
    return cos, sin


def _channel_dft_table():
    c = np.arange(F_CH)
    ang = 2.0 * np.pi * np.outer(c, c) / F_CH
    cs, sn = np.cos(ang) / math.sqrt(F_CH), np.sin(ang) / math.sqrt(F_CH)
    t = np.zeros((F_WIDTH, 2 * F_WIDTH), np.float64)
    for g in range(F_GROUPS):
        sl = slice(g * F_CH, (g + 1) * F_CH)
        t[sl, sl] = cs
        t[sl, F_WIDTH + g * F_CH:F_WIDTH + (g + 1) * F_CH] = -sn
    return jnp.asarray(t, dtype=F32).astype(BF16)


def _stage1_table(n1):
    k = np.arange(n1)
    ang = 2.0 * np.pi * np.outer(k, k) / n1
    cs, sn = np.cos(ang) / math.sqrt(n1), np.sin(ang) / math.sqrt(n1)
    return jnp.asarray(np.block([[cs, sn], [-sn, cs]]), dtype=F32).astype(BF16)


def _stage2_tables(n1):
    n2 = FFT_N2
    n = n1 * n2
    alpha = 2.0 * np.pi * np.outer(np.arange(n1), np.arange(n2)) / n
    beta = 2.0 * np.pi * np.outer(np.arange(n2), np.arange(n2)) / n2
    s = 1.0 / math.sqrt(n2)
    as_f32 = lambda a: jnp.asarray(a, dtype=F32)
    return as_f32(np.cos(alpha)), as_f32(np.sin(alpha)), as_f32(np.cos(beta) * s), as_f32(np.sin(beta) * s)


def _in_proj_kernel(x_ref, gmix_ref, win_ref, dftc_ref, gq_ref, gk_ref, cos_ref, sin_ref,
                    ur_ref, ui_ref, q_ref, k_ref, vt_ref, *, q_scale, row_splits):
    tm = x_ref.shape[0]
    sub = tm // row_splits
    lane = lax.broadcasted_iota(jnp.int32, (sub, HEAD_DIM), 1)
    first_half = (lane % ROPE_AXIS_DIM) < (ROPE_AXIS_DIM // 2)
    gq = gq_ref[...] * q_scale
    gk = gk_ref[...]

    for r in range(row_splits):
        rows = slice(r * sub, (r + 1) * sub)
        h = _rms(x_ref[rows, :], gmix_ref[...]).astype(BF16)
        z = jnp.dot(h, win_ref[...], preferred_element_type=F32)

        u = jnp.dot(z[:, :F_WIDTH].astype(BF16), dftc_ref[...], preferred_element_type=F32)
        ur_ref[rows, :] = u[:, :F_WIDTH].astype(BF16)
        ui_ref[rows, :] = u[:, F_WIDTH:].astype(BF16)

        cos = cos_ref[rows, :]
        sin = sin_ref[rows, :]

        def norm_rope(zh, g):
            y = _rms(zh, g)
            partner = jnp.where(first_half,
                                pltpu.roll(y, HEAD_DIM - ROPE_AXIS_DIM // 2, 1),
                                pltpu.roll(y, ROPE_AXIS_DIM // 2, 1))
            return y * cos + partner * sin

        for hh in range(N_Q_HEADS):
            lo = F_WIDTH + hh * HEAD_DIM
            q = norm_rope(z[:, lo:lo + HEAD_DIM], gq)
            q_ref[rows, hh * HEAD_DIM:(hh + 1) * HEAD_DIM] = q.astype(BF16)
        for hh in range(N_KV_HEADS):
            lo = F_WIDTH + Q_WIDTH + hh * HEAD_DIM
            k_ref[rows, hh * HEAD_DIM:(hh + 1) * HEAD_DIM] = norm_rope(z[:, lo:lo + HEAD_DIM], gk).astype(BF16)
            lo = F_WIDTH + Q_WIDTH + KV_WIDTH + hh * HEAD_DIM
            vt_ref[hh * HEAD_DIM:(hh + 1) * HEAD_DIM, rows] = z[:, lo:lo + HEAD_DIM].T.astype(BF16)


def _in_proj(x2d, batch, seq, g_mix, w_in, dftc, g_q, g_k, cos, sin, tm=512):
    t, d = x2d.shape
    nb = seq // tm
    in_w = w_in.shape[1]
    row = lambda i: (i, 0)
    pos = lambda i: (i % nb, 0)
    tr = lambda i: (i // nb, 0, i % nb)
    q_scale = LOG2E / math.sqrt(HEAD_DIM)
    return pl.pallas_call(
        functools.partial(_in_proj_kernel, q_scale=q_scale, row_splits=2),
        grid=(t // tm,),
        in_specs=[
            pl.BlockSpec((tm, d), row),
            _const_spec((1, d)),
            _const_spec((d, in_w)),
            _const_spec((F_WIDTH, 2 * F_WIDTH)),
            _const_spec((1, HEAD_DIM)),
            _const_spec((1, HEAD_DIM)),
            pl.BlockSpec((tm, HEAD_DIM), pos),
            pl.BlockSpec((tm, HEAD_DIM), pos),
        ],
        out_specs=[
            pl.BlockSpec((tm, F_WIDTH), row),
            pl.BlockSpec((tm, F_WIDTH), row),
            pl.BlockSpec((tm, Q_WIDTH), row),
            pl.BlockSpec((tm, KV_WIDTH), row),
            pl.BlockSpec((None, KV_WIDTH, tm), tr),
        ],
        out_shape=[
            jax.ShapeDtypeStruct((t, F_WIDTH), BF16),
            jax.ShapeDtypeStruct((t, F_WIDTH), BF16),
            jax.ShapeDtypeStruct((t, Q_WIDTH), BF16),
            jax.ShapeDtypeStruct((t, KV_WIDTH), BF16),
            jax.ShapeDtypeStruct((batch, KV_WIDTH, seq), BF16),
        ],
        compiler_params=_params("parallel"),
        name="in_proj",
    )(x2d, g_mix, w_in, dftc, g_q, g_k, cos, sin)


def _fourier1_kernel(m1_ref, ur_ref, ui_ref, yr_ref, yi_ref):
    n1 = ur_ref.shape[0]
    u = jnp.concatenate([ur_ref[...], ui_ref[...]], axis=0)
    y = jnp.dot(m1_ref[...], u, preferred_element_type=F32)
    yr_ref[...] = y[:n1].astype(BF16)
    yi_ref[...] = y[n1:].astype(BF16)


def _fourier2_kernel(ca_ref, sa_ref, cb_ref, sb_ref, wbd_ref, yr_ref, yi_ref, o_ref, *, kb):
    n2 = FFT_N2
    g = pl.program_id(1)
    cb = cb_ref[...]
    sb = sb_ref[...]
    wbd = wbd_ref[...]
    for j in range(kb):
        k1 = g * kb + j
        ca = ca_ref[pl.ds(k1, 1), :]
        sa = sa_ref[pl.ds(k1, 1), :]
        tw = jnp.concatenate([cb * ca - sb * sa, sb * ca + cb * sa], axis=1).astype(BF16)
        yy = jnp.concatenate([yr_ref[j * n2:(j + 1) * n2, :], yi_ref[j * n2:(j + 1) * n2, :]], axis=0)
        fr = jnp.dot(tw, yy, preferred_element_type=F32)
        o = jnp.dot(fr.astype(BF16), wbd, preferred_element_type=F32)
        o_ref[:, j * F_WIDTH:(j + 1) * F_WIDTH] = o.astype(BF16)


def _fourier(ur, ui, batch, seq, wbd, kb=8):
    n2 = FFT_N2
    n1 = seq // n2
    cols = n2 * F_WIDTH
    tc = min(cols, (4 * 1024 * 1024) // (2 * n1 * 4))
    m1 = _stage1_table(n1)
    ca, sa, cb, sb = _stage2_tables(n1)
    urv = ur.reshape(batch, n1, cols)
    uiv = ui.reshape(batch, n1, cols)
    blk1 = pl.BlockSpec((None, n1, tc), lambda b, c: (b, 0, c))
    yr, yi = pl.pallas_call(
        _fourier1_kernel,
        grid=(batch, cols // tc),
        in_specs=[_const_spec((2 * n1, 2 * n1)), blk1, blk1],
        out_specs=[blk1, blk1],
        out_shape=[jax.ShapeDtypeStruct((batch, n1, cols), BF16)] * 2,
        compiler_params=_params("parallel", "parallel"),
        name="fourier_stage1",
    )(m1, urv, uiv)

    kb = min(kb, n1)
    yblk = pl.BlockSpec((None, kb * n2, F_WIDTH), lambda b, g: (b, g, 0))
    out = pl.pallas_call(
        functools.partial(_fourier2_kernel, kb=kb),
        grid=(batch, n1 // kb),
        in_specs=[_const_spec((n1, n2)), _const_spec((n1, n2)), _const_spec((n2, n2)), _const_spec((n2, n2)),
                  _const_spec((F_WIDTH, F_WIDTH)), yblk, yblk],
        out_specs=pl.BlockSpec((None, n2, kb * F_WIDTH), lambda b, g: (b, 0, g)),
        out_shape=jax.ShapeDtypeStruct((batch, n2, n1 * F_WIDTH), BF16),
        compiler_params=_params("parallel", "parallel"),
        name="fourier_stage2",
    )(ca, sa, cb, sb, wbd, yr.reshape(batch, seq, F_WIDTH), yi.reshape(batch, seq, F_WIDTH))
    return out.reshape(batch * seq, F_WIDTH)


def _scores(k, q_ref, g):
    q = q_ref[:, g * HEAD_DIM:(g + 1) * HEAD_DIM]
    return lax.dot_general(k, q, (((1,), (1,)), ((), ())), preferred_element_type=F32)


def _attention_kernel(q_ref, k_ref, vt_ref, o_ref, m_ref, l_ref, acc_ref):
    j = pl.program_id(3)

    @pl.when(j == 0)
    def _():
        m_ref[...] = jnp.full(m_ref.shape, -jnp.inf, F32)
        l_ref[...] = jnp.zeros(l_ref.shape, F32)
        acc_ref[...] = jnp.zeros(acc_ref.shape, F32)

    k = k_ref[...]
    vt = vt_ref[...]
    for g in range(Q_PER_KV):
        s = _scores(k, q_ref, g)
        m_prev = m_ref[g]
        m_new = jnp.maximum(m_prev, jnp.max(s, axis=0, keepdims=True))
        alpha = jnp.exp2(m_prev - m_new)
        p = jnp.exp2(s - m_new)
        l_ref[g] = alpha * l_ref[g] + jnp.sum(p, axis=0, keepdims=True)
        acc_ref[g] = alpha * acc_ref[g] + jnp.dot(vt, p.astype(BF16), preferred_element_type=F32)
        m_ref[g] = m_new

    @pl.when(j == pl.num_programs(3) - 1)
    def _():
        for g in range(Q_PER_KV):
            o = acc_ref[g] * (1.0 / l_ref[g])
            o_ref[:, g * HEAD_DIM:(g + 1) * HEAD_DIM] = o.T.astype(BF16)


def _attention_bounded_kernel(q_ref, k_ref, vt_ref, o_ref, l_ref, acc_ref):
    j = pl.program_id(3)

    @pl.when(j == 0)
    def _():
        l_ref[...] = jnp.zeros(l_ref.shape, F32)
        acc_ref[...] = jnp.zeros(acc_ref.shape, F32)

    k = k_ref[...]
    vt = vt_ref[...]
    tk = k.shape[0]
    for g in range(Q_PER_KV):
        p = jnp.exp2(_scores(k, q_ref, g))
        l_ref[g] += jnp.sum(p.reshape(tk // SUBLANES, SUBLANES, p.shape[1]), axis=0)
        acc_ref[g] += jnp.dot(vt, p.astype(BF16), preferred_element_type=F32)

    @pl.when(j == pl.num_programs(3) - 1)
    def _():
        for g in range(Q_PER_KV):
            denom = jnp.sum(l_ref[g], axis=0, keepdims=True)
            o = acc_ref[g] * (1.0 / denom)
            o_ref[:, g * HEAD_DIM:(g + 1) * HEAD_DIM] = o.T.astype(BF16)


def _attention_call(body, scratch, q, k, vt, tq, tk, name):
    batch, seq, _ = q.shape
    tq, tk = min(tq, seq), min(tk, seq)
    gw = Q_PER_KV * HEAD_DIM
    return pl.pallas_call(
        body,
        grid=(batch, N_KV_HEADS, seq // tq, seq // tk),
        in_specs=[
            pl.BlockSpec((None, tq, gw), lambda b, h, i, j: (b, i, h)),
            pl.BlockSpec((None, tk, HEAD_DIM), lambda b, h, i, j: (b, j, h)),
            pl.BlockSpec((None, HEAD_DIM, tk), lambda b, h, i, j: (b, h, j)),
        ],
        out_specs=pl.BlockSpec((None, tq, gw), lambda b, h, i, j: (b, i, h)),
        out_shape=jax.ShapeDtypeStruct((batch, seq, Q_WIDTH), BF16),
        scratch_shapes=scratch(tq),
        compiler_params=_params("parallel", "parallel", "parallel", "arbitrary"),
        name=name,
    )(q, k, vt)


def _attention(q, k, vt, score_bound):
    online_scratch = lambda tq: [pltpu.VMEM((Q_PER_KV, 1, tq), F32), pltpu.VMEM((Q_PER_KV, 1, tq), F32),
                                 pltpu.VMEM((Q_PER_KV, HEAD_DIM, tq), F32)]
    bounded_scratch = lambda tq: [pltpu.VMEM((Q_PER_KV, SUBLANES, tq), F32),
                                  pltpu.VMEM((Q_PER_KV, HEAD_DIM, tq), F32)]
    bounded = functools.partial(_attention_call, _attention_bounded_kernel, bounded_scratch,
                                tq=512, tk=8192, name="self_attention_bounded")
    online = functools.partial(_attention_call, _attention_kernel, online_scratch,
                               tq=512, tk=512, name="self_attention_online")
    return lax.cond(score_bound <= SAFE_SCORE_BOUND, bounded, online, q, k, vt)


def _mem_kv_kernel(mem_ref, gmem_ref, wckv_ref, gck_ref, kt_ref, v_ref):
    d = mem_ref.shape[1]
    xh = d // X_HEADS
    h = _rms(mem_ref[...], gmem_ref[...]).astype(BF16)
    kv = jnp.dot(h, wckv_ref[...], preferred_element_type=F32)
    gck = gck_ref[...]
    for hh in range(X_HEADS):
        kh = _rms(kv[:, hh * xh:(hh + 1) * xh], gck)
        kt_ref[hh * xh:(hh + 1) * xh, :] = kh.T.astype(BF16)
    v_ref[...] = kv[:, d:].astype(BF16)


def _mem_kv(mem, g_mem, w_ckv, g_ck):
    batch, m, d = mem.shape
    return pl.pallas_call(
        _mem_kv_kernel,
        grid=(batch,),
        in_specs=[pl.BlockSpec((None, m, d), lambda b: (b, 0, 0)),
                  _const_spec((1, d)), _const_spec((d, 2 * d)), _const_spec((1, d // X_HEADS))],
        out_specs=[pl.BlockSpec((None, d, m), lambda b: (b, 0, 0)),
                   pl.BlockSpec((None, m, d), lambda b: (b, 0, 0))],
        out_shape=[jax.ShapeDtypeStruct((batch, d, m), BF16), jax.ShapeDtypeStruct((batch, m, d), BF16)],
        compiler_params=_params("parallel"),
        name="mem_kv",
    )(mem, g_mem, w_ckv, g_ck)


def _post_kernel(x_ref, f_ref, a_ref, wout_ref, gcross_ref, wcq_ref, gcq_ref, kt_ref, v_ref, wco_ref,
                 gmlp_ref, wup_ref, wdown_ref, y_ref, *, x_scale, ff_chunk):
    d = x_ref.shape[1]
    xh = d // X_HEADS
    x1 = (x_ref[...]
          + jnp.dot(f_ref[...], wout_ref[:F_WIDTH, :], preferred_element_type=F32)
          + jnp.dot(a_ref[...], wout_ref[F_WIDTH:, :], preferred_element_type=F32))

    qc = jnp.dot(_rms(x1, gcross_ref[...]).astype(BF16), wcq_ref[...], preferred_element_type=F32)
    gcq = gcq_ref[...]
    heads = []
    for hh in range(X_HEADS):
        sl = slice(hh * xh, (hh + 1) * xh)
        qh = (_rms(qc[:, sl], gcq) * x_scale).astype(BF16)
        s = jnp.dot(qh, kt_ref[sl, :], preferred_element_type=F32)
        p = jnp.exp2(s - jnp.max(s, axis=-1, keepdims=True))
        o = jnp.dot(p.astype(BF16), v_ref[:, sl], preferred_element_type=F32)
        heads.append((o * (1.0 / jnp.sum(p, axis=-1, keepdims=True))).astype(BF16))
    x2 = x1 + jnp.dot(jnp.concatenate(heads, axis=-1), wco_ref[...], preferred_element_type=F32)

    h3 = _rms(x2, gmlp_ref[...]).astype(BF16)
    acc = x2
    for c in range(wup_ref.shape[1] // ff_chunk):
        sl = slice(c * ff_chunk, (c + 1) * ff_chunk)
        u = jnp.maximum(jnp.dot(h3, wup_ref[:, sl], preferred_element_type=F32), 0.0)
        acc = acc + jnp.dot((u * u).astype(BF16), wdown_ref[sl, :], preferred_element_type=F32)
    y_ref[...] = acc


def _post(x2d, f_out, a_out, kt, v, seq, w_out, g_cross, w_cq, g_cq, w_co, g_mlp, w_up, w_down, tm=512):
    t, d = x2d.shape
    nb = seq // tm
    m = v.shape[1]
    d_ff = w_up.shape[1]
    row = lambda i: (i, 0)
    x_scale = LOG2E / math.sqrt(d // X_HEADS)
    return pl.pallas_call(
        functools.partial(_post_kernel, x_scale=x_scale, ff_chunk=1024),
        grid=(t // tm,),
        in_specs=[
            pl.BlockSpec((tm, d), row),
            pl.BlockSpec((tm, F_WIDTH), row),
            pl.BlockSpec((tm, Q_WIDTH), row),
            _const_spec((d, d)),
            _const_spec((1, d)),
            _const_spec((d, d)),
            _const_spec((1, d // X_HEADS)),
            pl.BlockSpec((None, d, m), lambda i: (i // nb, 0, 0)),
            pl.BlockSpec((None, m, d), lambda i: (i // nb, 0, 0)),
            _const_spec((d, d)),
            _const_spec((1, d)),
            _const_spec((d, d_ff)),
            _const_spec((d_ff, d)),
        ],
        out_specs=pl.BlockSpec((tm, d), row),
        out_shape=jax.ShapeDtypeStruct((t, d), F32),
        compiler_params=_params("parallel"),
        name="post",
    )(x2d, f_out, a_out, w_out, g_cross, w_cq, g_cq, kt, v, w_co, g_mlp, w_up, w_down)


def _block_diag_weights(w_f):
    out = jnp.zeros((F_WIDTH, F_WIDTH), w_f.dtype)
    for g in range(F_GROUPS):
        out = out.at[g * F_CH:(g + 1) * F_CH, g * F_CH:(g + 1) * F_CH].set(w_f[g])
    return out


def _layer(x, mem, cos, sin, dftc, p):
    batch, seq, d = x.shape
    x2d = x.reshape(batch * seq, d)
    ur, ui, q, k, vt = _in_proj(x2d, batch, seq, p["g_mix"], p["w_in"], dftc, p["g_q"], p["g_k"],
                                cos[:seq], sin[:seq])
    f_out = _fourier(ur, ui, batch, seq, p["wbd"])
    score_bound = (HEAD_DIM * LOG2E / math.sqrt(HEAD_DIM)) * jnp.max(jnp.abs(p["g_q"])) * jnp.max(jnp.abs(p["g_k"]))
    a_out = _attention(q.reshape(batch, seq, Q_WIDTH), k.reshape(batch, seq, KV_WIDTH), vt,
                       score_bound).reshape(batch * seq, Q_WIDTH)
    kt, v = _mem_kv(mem, p["g_mem"], p["w_ckv"], p["g_ck"])
    y = _post(x2d, f_out, a_out, kt, v, seq, p["w_out"], p["g_cross"], p["w_cq"], p["g_cq"], p["w_co"],
              p["g_mlp"], p["w_up"], p["w_down"])
    return y.reshape(batch, seq, d)


def kernel(x_prompt, x_sample, mem_prompt, mem_sample, g_mix, w_in, w_fourier, g_q, g_k, w_out, g_cross, g_mem,
           w_cq, w_ckv, g_cq, g_ck, w_co, g_mlp, w_up, w_down):
    depth = w_in.shape[0]
    max_seq = max(x_prompt.shape[1], x_sample.shape[1])
    cos, sin = _rope_tables(max_seq)
    dftc = _channel_dft_table()
    yp, ys = x_prompt, x_sample
    for l in range(depth):
        row = lambda g: g[l][None, :]
        p = dict(
            g_mix=row(g_mix), w_in=w_in[l].astype(BF16), wbd=_block_diag_weights(w_fourier[l]).astype(BF16),
            g_q=row(g_q), g_k=row(g_k), w_out=w_out[l].astype(BF16), g_cross=row(g_cross), g_mem=row(g_mem),
            w_cq=w_cq[l].astype(BF16), w_ckv=w_ckv[l].astype(BF16), g_cq=row(g_cq), g_ck=row(g_ck),
            w_co=w_co[l].astype(BF16), g_mlp=row(g_mlp), w_up=w_up[l].astype(BF16), w_down=w_down[l].astype(BF16),
        )
        yp = _layer(yp, mem_prompt, cos, sin, dftc, p)
        ys = _layer(ys, mem_sample, cos, sin, dftc, p)
    return (yp, ys)
```

```python
import functools
import math

import numpy as np
import jax
import jax.numpy as jnp
from jax import lax
from jax.experimental import pallas as pl
from jax.experimental.pallas import tpu as pltpu

F32 = jnp.float32
BF16 = jnp.bfloat16

GRID_W = 64
F_GROUPS = 4
F_CH = 64
F_WIDTH = F_GROUPS * F_CH
HEAD_DIM = 128
N_Q_HEADS = 6
N_KV_HEADS = 2
Q_PER_KV = N_Q_HEADS // N_KV_HEADS
Q_WIDTH = N_Q_HEADS * HEAD_DIM
KV_WIDTH = N_KV_HEADS * HEAD_DIM
ROPE_AXIS_DIM = HEAD_DIM // 2
ROPE_THETA = 10000.0
X_HEADS = 4
EPS = 1e-6
LOG2E = 1.4426950408889634

VMEM_LIMIT_BYTES = 56 * 1024 * 1024
FFT_N2 = 128
SAFE_SCORE_BOUND = 60.0
SUBLANES = 8


def _params(*sem):
    return pltpu.CompilerParams(dimension_semantics=sem, vmem_limit_bytes=VMEM_LIMIT_BYTES)


def _const_spec(shape):
    nd = len(shape)
    return pl.BlockSpec(shape, lambda *_: (0,) * nd, pipeline_mode=pl.Buffered(1))


def _rms(x, g):
    ms = jnp.mean(x * x, axis=-1, keepdims=True)
    return x * lax.rsqrt(ms + EPS) * g


def _rope_tables(seq):
    rows = seq // GRID_W
    inv_freq = 1.0 / (ROPE_THETA ** (jnp.arange(0, ROPE_AXIS_DIM, 2, dtype=F32) / ROPE_AXIS_DIM))
    ang_r = jnp.arange(rows, dtype=F32)[:, None] * inv_freq[None, :]
    ang_c = jnp.arange(GRID_W, dtype=F32)[:, None] * inv_freq[None, :]

    def expand_r(a):
        return jnp.repeat(a, GRID_W, axis=0)

    def expand_c(a):
        return jnp.tile(a, (rows, 1))

    cr, sr = expand_r(jnp.cos(ang_r)), expand_r(jnp.sin(ang_r))
    cc, sc = expand_c(jnp.cos(ang_c)), expand_c(jnp.sin(ang_c))
    cos = jnp.concatenate([cr, cr, cc, cc], axis=-1)
    sin = jnp.concatenate([-sr, sr, -sc, sc], axis=-1)
    return cos, sin


def _channel_dft_table():
    c = np.arange(F_CH)
    ang = 2.0 * np.pi * np.outer(c, c) / F_CH
    cs, sn = np.cos(ang) / math.sqrt(F_CH), np.sin(ang) / math.sqrt(F_CH)
    t = np.zeros((F_WIDTH, 2 * F_WIDTH), np.float64)
    for g in range(F_GROUPS):
        sl = slice(g * F_CH, (g + 1) * F_CH)
        t[sl, sl] = cs
        t[sl, F_WIDTH + g * F_CH:F_WIDTH + (g + 1) * F_CH] = -sn
    return jnp.asarray(t, dtype=F32).astype(BF16)


def _stage1_table(n1):
    k = np.arange(n1)
    ang = 2.0 * np.pi * np.outer(k, k) / n1
    cs, sn = np.cos(ang) / math.sqrt(n1), np.sin(ang) / math.sqrt(n1)
    return jnp.asarray(np.block([[cs, sn], [-sn, cs]]), dtype=F32).astype(BF16)


def _stage2_tables(n1):
    n2 = FFT_N2
    n = n1 * n2
    alpha = 2.0 * np.pi * np.outer(np.arange(n1), np.arange(n2)) / n
    beta = 2.0 * np.pi * np.outer(np.arange(n2), np.arange(n2)) / n2
    s = 1.0 / math.sqrt(n2)
    as_f32 = lambda a: jnp.asarray(a, dtype=F32)
    return as_f32(np.cos(alpha)), as_f32(np.sin(alpha)), as_f32(np.cos(beta) * s), as_f32(np.sin(beta) * s)


def _in_proj_kernel(x_ref, gmix_ref, win_ref, dftc_ref, gq_ref, gk_ref, cos_ref, sin_ref,
                    ur_ref, ui_ref, q_ref, k_ref, v_ref, *, q_scale, row_splits):
    tm = x_ref.shape[0]
    sub = tm // row_splits
    lane = lax.broadcasted_iota(jnp.int32, (sub, HEAD_DIM), 1)
    first_half = (lane % ROPE_AXIS_DIM) < (ROPE_AXIS_DIM // 2)
    gq = gq_ref[...] * q_scale
    gk = gk_ref[...]

    for r in range(row_splits):
        rows = slice(r * sub, (r + 1) * sub)
        h = _rms(x_ref[rows, :], gmix_ref[...]).astype(BF16)
        z = jnp.dot(h, win_ref[...], preferred_element_type=F32)

        u = jnp.dot(z[:, :F_WIDTH].astype(BF16), dftc_ref[...], preferred_element_type=F32)
        ur_ref[rows, :] = u[:, :F_WIDTH].astype(BF16)
        ui_ref[rows, :] = u[:, F_WIDTH:].astype(BF16)

        cos = cos_ref[rows, :]
        sin = sin_ref[rows, :]

        def norm_rope(zh, g):
            y = _rms(zh, g)
            partner = jnp.where(first_half,
                                pltpu.roll(y, HEAD_DIM - ROPE_AXIS_DIM // 2, 1),
                                pltpu.roll(y, ROPE_AXIS_DIM // 2, 1))
            return y * cos + partner * sin

        for hh in range(N_Q_HEADS):
            lo = F_WIDTH + hh * HEAD_DIM
            q = norm_rope(z[:, lo:lo + HEAD_DIM], gq)
            q_ref[rows, hh * HEAD_DIM:(hh + 1) * HEAD_DIM] = q.astype(BF16)
        for hh in range(N_KV_HEADS):
            lo = F_WIDTH + Q_WIDTH + hh * HEAD_DIM
            k_ref[rows, hh * HEAD_DIM:(hh + 1) * HEAD_DIM] = norm_rope(z[:, lo:lo + HEAD_DIM], gk).astype(BF16)
            lo = F_WIDTH + Q_WIDTH + KV_WIDTH + hh * HEAD_DIM
            v_ref[rows, hh * HEAD_DIM:(hh + 1) * HEAD_DIM] = z[:, lo:lo + HEAD_DIM].astype(BF16)


def _in_proj(x2d, batch, seq, g_mix, w_in, dftc, g_q, g_k, cos, sin, tm=512):
    t, d = x2d.shape
    nb = seq // tm
    in_w = w_in.shape[1]
    row = lambda i: (i, 0)
    pos = lambda i: (i % nb, 0)
    q_scale = LOG2E / math.sqrt(HEAD_DIM)
    return pl.pallas_call(
        functools.partial(_in_proj_kernel, q_scale=q_scale, row_splits=2),
        grid=(t // tm,),
        in_specs=[
            pl.BlockSpec((tm, d), row),
            _const_spec((1, d)),
            _const_spec((d, in_w)),
            _const_spec((F_WIDTH, 2 * F_WIDTH)),
            _const_spec((1, HEAD_DIM)),
            _const_spec((1, HEAD_DIM)),
            pl.BlockSpec((tm, HEAD_DIM), pos),
            pl.BlockSpec((tm, HEAD_DIM), pos),
        ],
        out_specs=[
            pl.BlockSpec((tm, F_WIDTH), row),
            pl.BlockSpec((tm, F_WIDTH), row),
            pl.BlockSpec((tm, Q_WIDTH), row),
            pl.BlockSpec((tm, KV_WIDTH), row),
            pl.BlockSpec((tm, KV_WIDTH), row),
        ],
        out_shape=[
            jax.ShapeDtypeStruct((t, F_WIDTH), BF16),
            jax.ShapeDtypeStruct((t, F_WIDTH), BF16),
            jax.ShapeDtypeStruct((t, Q_WIDTH), BF16),
            jax.ShapeDtypeStruct((t, KV_WIDTH), BF16),
            jax.ShapeDtypeStruct((t, KV_WIDTH), BF16),
        ],
        compiler_params=_params("parallel"),
        name="in_proj",
    )(x2d, g_mix, w_in, dftc, g_q, g_k, cos, sin)


def _fourier1_kernel(m1_ref, ur_ref, ui_ref, yr_ref, yi_ref):
    n1 = ur_ref.shape[0]
    u = jnp.concatenate([ur_ref[...], ui_ref[...]], axis=0)
    y = jnp.dot(m1_ref[...], u, preferred_element_type=F32)
    yr_ref[...] = y[:n1].astype(BF16)
    yi_ref[...] = y[n1:].astype(BF16)


def _fourier2_kernel(ca_ref, sa_ref, cb_ref, sb_ref, wbd_ref, yr_ref, yi_ref, o_ref, *, kb):
    n2 = FFT_N2
    g = pl.program_id(1)
    cb = cb_ref[...]
    sb = sb_ref[...]
    wbd = wbd_ref[...]
    for j in range(kb):
        k1 = g * kb + j
        ca = ca_ref[pl.ds(k1, 1), :]
        sa = sa_ref[pl.ds(k1, 1), :]
        tw = jnp.concatenate([cb * ca - sb * sa, sb * ca + cb * sa], axis=1).astype(BF16)
        yy = jnp.concatenate([yr_ref[j * n2:(j + 1) * n2, :], yi_ref[j * n2:(j + 1) * n2, :]], axis=0)
        fr = jnp.dot(tw, yy, preferred_element_type=F32)
        o = jnp.dot(fr.astype(BF16), wbd, preferred_element_type=F32)
        o_ref[:, j * F_WIDTH:(j + 1) * F_WIDTH] = o.astype(BF16)


def _fourier(ur, ui, batch, seq, wbd, kb=8):
    n2 = FFT_N2
    n1 = seq // n2
    cols = n2 * F_WIDTH
    tc = min(cols, (4 * 1024 * 1024) // (2 * n1 * 4))
    m1 = _stage1_table(n1)
    ca, sa, cb, sb = _stage2_tables(n1)
    urv = ur.reshape(batch, n1, cols)
    uiv = ui.reshape(batch, n1, cols)
    blk1 = pl.BlockSpec((None, n1, tc), lambda b, c: (b, 0, c))
    yr, yi = pl.pallas_call(
        _fourier1_kernel,
        grid=(batch, cols // tc),
        in_specs=[_const_spec((2 * n1, 2 * n1)), blk1, blk1],
        out_specs=[blk1, blk1],
        out_shape=[jax.ShapeDtypeStruct((batch, n1, cols), BF16)] * 2,
        compiler_params=_params("parallel", "parallel"),
        name="fourier_stage1",
    )(m1, urv, uiv)

    kb = min(kb, n1)
    yblk = pl.BlockSpec((None, kb * n2, F_WIDTH), lambda b, g: (b, g, 0))
    out = pl.pallas_call(
        functools.partial(_fourier2_kernel, kb=kb),
        grid=(batch, n1 // kb),
        in_specs=[_const_spec((n1, n2)), _const_spec((n1, n2)), _const_spec((n2, n2)), _const_spec((n2, n2)),
                  _const_spec((F_WIDTH, F_WIDTH)), yblk, yblk],
        out_specs=pl.BlockSpec((None, n2, kb * F_WIDTH), lambda b, g: (b, 0, g)),
        out_shape=jax.ShapeDtypeStruct((batch, n2, n1 * F_WIDTH), BF16),
        compiler_params=_params("parallel", "parallel"),
        name="fourier_stage2",
    )(ca, sa, cb, sb, wbd, yr.reshape(batch, seq, F_WIDTH), yi.reshape(batch, seq, F_WIDTH))
    return out.reshape(batch * seq, F_WIDTH)


def _scores(k, q_ref, g):
    q = q_ref[:, g * HEAD_DIM:(g + 1) * HEAD_DIM]
    return lax.dot_general(k, q, (((1,), (1,)), ((), ())), preferred_element_type=F32)


def _attention_kernel(q_ref, k_ref, v_ref, o_ref, m_ref, l_ref, acc_ref):
    j = pl.program_id(3)

    @pl.when(j == 0)
    def _():
        m_ref[...] = jnp.full(m_ref.shape, -jnp.inf, F32)
        l_ref[...] = jnp.zeros(l_ref.shape, F32)
        acc_ref[...] = jnp.zeros(acc_ref.shape, F32)

    k = k_ref[...]
    vt = v_ref[...].T
    for g in range(Q_PER_KV):
        s = _scores(k, q_ref, g)
        m_prev = m_ref[g]
        m_new = jnp.maximum(m_prev, jnp.max(s, axis=0, keepdims=True))
        alpha = jnp.exp2(m_prev - m_new)
        p = jnp.exp2(s - m_new)
        l_ref[g] = alpha * l_ref[g] + jnp.sum(p, axis=0, keepdims=True)
        acc_ref[g] = alpha * acc_ref[g] + jnp.dot(vt, p.astype(BF16), preferred_element_type=F32)
        m_ref[g] = m_new

    @pl.when(j == pl.num_programs(3) - 1)
    def _():
        for g in range(Q_PER_KV):
            o = acc_ref[g] * (1.0 / l_ref[g])
            o_ref[:, g * HEAD_DIM:(g + 1) * HEAD_DIM] = o.T.astype(BF16)


def _attention_bounded_kernel(q_ref, k_ref, v_ref, o_ref, l_ref, acc_ref):
    j = pl.program_id(3)

    @pl.when(j == 0)
    def _():
        l_ref[...] = jnp.zeros(l_ref.shape, F32)
        acc_ref[...] = jnp.zeros(acc_ref.shape, F32)

    k = k_ref[...]
    vt = v_ref[...].T
    tk = k.shape[0]
    for g in range(Q_PER_KV):
        p = jnp.exp2(_scores(k, q_ref, g))
        l_ref[g] += jnp.sum(p.reshape(tk // SUBLANES, SUBLANES, p.shape[1]), axis=0)
        acc_ref[g] += jnp.dot(vt, p.astype(BF16), preferred_element_type=F32)

    @pl.when(j == pl.num_programs(3) - 1)
    def _():
        for g in range(Q_PER_KV):
            denom = jnp.sum(l_ref[g], axis=0, keepdims=True)
            o = acc_ref[g] * (1.0 / denom)
            o_ref[:, g * HEAD_DIM:(g + 1) * HEAD_DIM] = o.T.astype(BF16)


def _attention_call(body, scratch, q, k, v, tq, tk, name):
    batch, seq, _ = q.shape
    tq, tk = min(tq, seq), min(tk, seq)
    gw = Q_PER_KV * HEAD_DIM
    return pl.pallas_call(
        body,
        grid=(batch, N_KV_HEADS, seq // tq, seq // tk),
        in_specs=[
            pl.BlockSpec((None, tq, gw), lambda b, h, i, j: (b, i, h)),
            pl.BlockSpec((None, tk, HEAD_DIM), lambda b, h, i, j: (b, j, h)),
            pl.BlockSpec((None, tk, HEAD_DIM), lambda b, h, i, j: (b, j, h)),
        ],
        out_specs=pl.BlockSpec((None, tq, gw), lambda b, h, i, j: (b, i, h)),
        out_shape=jax.ShapeDtypeStruct((batch, seq, Q_WIDTH), BF16),
        scratch_shapes=scratch(tq),
        compiler_params=_params("parallel", "parallel", "parallel", "arbitrary"),
        name=name,
    )(q, k, v)


def _attention(q, k, v, score_bound):
    online_scratch = lambda tq: [pltpu.VMEM((Q_PER_KV, 1, tq), F32), pltpu.VMEM((Q_PER_KV, 1, tq), F32),
                                 pltpu.VMEM((Q_PER_KV, HEAD_DIM, tq), F32)]
    bounded_scratch = lambda tq: [pltpu.VMEM((Q_PER_KV, SUBLANES, tq), F32),
                                  pltpu.VMEM((Q_PER_KV, HEAD_DIM, tq), F32)]
    bounded = functools.partial(_attention_call, _attention_bounded_kernel, bounded_scratch,
                                tq=512, tk=8192, name="self_attention_bounded")
    online = functools.partial(_attention_call, _attention_kernel, online_scratch,
                               tq=512, tk=512, name="self_attention_online")
    return lax.cond(score_bound <= SAFE_SCORE_BOUND, bounded, online, q, k, v)


def _mem_kv_kernel(mem_ref, gmem_ref, wckv_ref, gck_ref, kt_ref, v_ref):
    d = mem_ref.shape[1]
    xh = d // X_HEADS
    h = _rms(mem_ref[...], gmem_ref[...]).astype(BF16)
    kv = jnp.dot(h, wckv_ref[...], preferred_element_type=F32)
    gck = gck_ref[...]
    for hh in range(X_HEADS):
        kh = _rms(kv[:, hh * xh:(hh + 1) * xh], gck)
        kt_ref[hh * xh:(hh + 1) * xh, :] = kh.T.astype(BF16)
    v_ref[...] = kv[:, d:].astype(BF16)


def _mem_kv(mem, g_mem, w_ckv, g_ck):
    batch, m, d = mem.shape
    return pl.pallas_call(
        _mem_kv_kernel,
        grid=(batch,),
        in_specs=[pl.BlockSpec((None, m, d), lambda b: (b, 0, 0)),
                  _const_spec((1, d)), _const_spec((d, 2 * d)), _const_spec((1, d // X_HEADS))],
        out_specs=[pl.BlockSpec((None, d, m), lambda b: (b, 0, 0)),
                   pl.BlockSpec((None, m, d), lambda b: (b, 0, 0))],
        out_shape=[jax.ShapeDtypeStruct((batch, d, m), BF16), jax.ShapeDtypeStruct((batch, m, d), BF16)],
        compiler_params=_params("parallel"),
        name="mem_kv",
    )(mem, g_mem, w_ckv, g_ck)


def _post_kernel(x_ref, f_ref, a_ref, wout_ref, gcross_ref, wcq_ref, gcq_ref, kt_ref, v_ref, wco_ref,
                 gmlp_ref, wup_ref, wdown_ref, y_ref, *, x_scale, ff_chunk):
    d = x_ref.shape[1]
    xh = d // X_HEADS
    x1 = (x_ref[...]
          + jnp.dot(f_ref[...], wout_ref[:F_WIDTH, :], preferred_element_type=F32)
          + jnp.dot(a_ref[...], wout_ref[F_WIDTH:, :], preferred_element_type=F32))

    qc = jnp.dot(_rms(x1, gcross_ref[...]).astype(BF16), wcq_ref[...], preferred_element_type=F32)
    gcq = gcq_ref[...]
    heads = []
    for hh in range(X_HEADS):
        sl = slice(hh * xh, (hh + 1) * xh)
        qh = (_rms(qc[:, sl], gcq) * x_scale).astype(BF16)
        s = jnp.dot(qh, kt_ref[sl, :], preferred_element_type=F32)
        p = jnp.exp2(s - jnp.max(s, axis=-1, keepdims=True))
        o = jnp.dot(p.astype(BF16), v_ref[:, sl], preferred_element_type=F32)
        heads.append((o * (1.0 / jnp.sum(p, axis=-1, keepdims=True))).astype(BF16))
    x2 = x1 + jnp.dot(jnp.concatenate(heads, axis=-1), wco_ref[...], preferred_element_type=F32)

    h3 = _rms(x2, gmlp_ref[...]).astype(BF16)
    acc = x2
    for c in range(wup_ref.shape[1] // ff_chunk):
        sl = slice(c * ff_chunk, (c + 1) * ff_chunk)
        u = jnp.maximum(jnp.dot(h3, wup_ref[:, sl], preferred_element_type=F32), 0.0)
        acc = acc + jnp.dot((u * u).astype(BF16), wdown_ref[sl, :], preferred_element_type=F32)
    y_ref[...] = acc


def _post(x2d, f_out, a_out, kt, v, seq, w_out, g_cross, w_cq, g_cq, w_co, g_mlp, w_up, w_down, tm=512):
    t, d = x2d.shape
    nb = seq // tm
    m = v.shape[1]
    d_ff = w_up.shape[1]
    row = lambda i: (i, 0)
    x_scale = LOG2E / math.sqrt(d // X_HEADS)
    return pl.pallas_call(
        functools.partial(_post_kernel, x_scale=x_scale, ff_chunk=1024),
        grid=(t // tm,),
        in_specs=[
            pl.BlockSpec((tm, d), row),
            pl.BlockSpec((tm, F_WIDTH), row),
            pl.BlockSpec((tm, Q_WIDTH), row),
            _const_spec((d, d)),
            _const_spec((1, d)),
            _const_spec((d, d)),
            _const_spec((1, d // X_HEADS)),
            pl.BlockSpec((None, d, m), lambda i: (i // nb, 0, 0)),
            pl.BlockSpec((None, m, d), lambda i: (i // nb, 0, 0)),
            _const_spec((d, d)),
            _const_spec((1, d)),
            _const_spec((d, d_ff)),
            _const_spec((d_ff, d)),
        ],
        out_specs=pl.BlockSpec((tm, d), row),
        out_shape=jax.ShapeDtypeStruct((t, d), F32),
        compiler_params=_params("parallel"),
        name="post",
    )(x2d, f_out, a_out, w_out, g_cross, w_cq, g_cq, kt, v, w_co, g_mlp, w_up, w_down)


def _block_diag_weights(w_f):
    out = jnp.zeros((F_WIDTH, F_WIDTH), w_f.dtype)
    for g in range(F_GROUPS):
        out = out.at[g * F_CH:(g + 1) * F_CH, g * F_CH:(g + 1) * F_CH].set(w_f[g])
    return out


def _layer(x, mem, cos, sin, dftc, p):
    batch, seq, d = x.shape
    x2d = x.reshape(batch * seq, d)
    ur, ui, q, k, v = _in_proj(x2d, batch, seq, p["g_mix"], p["w_in"], dftc, p["g_q"], p["g_k"],
                                cos[:seq], sin[:seq])
    f_out = _fourier(ur, ui, batch, seq, p["wbd"])
    score_bound = (HEAD_DIM * LOG2E / math.sqrt(HEAD_DIM)) * jnp.max(jnp.abs(p["g_q"])) * jnp.max(jnp.abs(p["g_k"]))
    a_out = _attention(q.reshape(batch, seq, Q_WIDTH), k.reshape(batch, seq, KV_WIDTH),
                       v.reshape(batch, seq, KV_WIDTH),
                       score_bound).reshape(batch * seq, Q_WIDTH)
    kt, v = _mem_kv(mem, p["g_mem"], p["w_ckv"], p["g_ck"])
    y = _post(x2d, f_out, a_out, kt, v, seq, p["w_out"], p["g_cross"], p["w_cq"], p["g_cq"], p["w_co"],
              p["g_mlp"], p["w_up"], p["w_down"])
    return y.reshape(batch, seq, d)


def kernel(x_prompt, x_sample, mem_prompt, mem_sample, g_mix, w_in, w_fourier, g_q, g_k, w_out, g_cross, g_mem,
           w_cq, w_ckv, g_cq, g_ck, w_co, g_mlp, w_up, w_down):
    depth = w_in.shape[0]
    max_seq = max(x_prompt.shape[1], x_sample.shape[1])
    cos, sin = _rope_tables(max_seq)
    dftc = _channel_dft_table()
    yp, ys = x_prompt, x_sample
    for l in range(depth):
        row = lambda g: g[l][None, :]
        p = dict(
            g_mix=row(g_mix), w_in=w_in[l].astype(BF16), wbd=_block_diag_weights(w_fourier[l]).astype(BF16),
            g_q=row(g_q), g_k=row(g_k), w_out=w_out[l].astype(BF16), g_cross=row(g_cross), g_mem=row(g_mem),
            w_cq=w_cq[l].astype(BF16), w_ckv=w_ckv[l].astype(BF16), g_cq=row(g_cq), g_ck=row(g_ck),
            w_co=w_co[l].astype(BF16), g_mlp=row(g_mlp), w_up=w_up[l].astype(BF16), w_down=w_down[l].astype(BF16),
        )
        yp = _layer(yp, mem_prompt, cos, sin, dftc, p)
        ys = _layer(ys, mem_sample, cos, sin, dftc, p)
    return (yp, ys)
```

```python
import functools
import math

import numpy as np
import jax
import jax.numpy as jnp
from jax import lax
from jax.experimental import pallas as pl
from jax.experimental.pallas import tpu as pltpu

F32 = jnp.float32
BF16 = jnp.bfloat16

GRID_W = 64
F_GROUPS = 4
F_CH = 64
F_WIDTH = F_GROUPS * F_CH
HEAD_DIM = 128
N_Q_HEADS = 6
N_KV_HEADS = 2
Q_PER_KV = N_Q_HEADS // N_KV_HEADS
Q_WIDTH = N_Q_HEADS * HEAD_DIM
KV_WIDTH = N_KV_HEADS * HEAD_DIM
ROPE_AXIS_DIM = HEAD_DIM // 2
ROPE_THETA = 10000.0
X_HEADS = 4
EPS = 1e-6
LOG2E = 1.4426950408889634

VMEM_LIMIT_BYTES = 56 * 1024 * 1024
FFT_N2 = 128
SAFE_SCORE_BOUND = 60.0
SUBLANES = 8
BOUNDED_KEY_TILE = 8192
BOUNDED_SCORE_TILE = 4 * 1024 * 1024


def _params(*sem):
    return pltpu.CompilerParams(dimension_semantics=sem, vmem_limit_bytes=VMEM_LIMIT_BYTES)


def _const_spec(shape):
    nd = len(shape)
    return pl.BlockSpec(shape, lambda *_: (0,) * nd, pipeline_mode=pl.Buffered(1))


def _rms(x, g):
    ms = jnp.mean(x * x, axis=-1, keepdims=True)
    return x * lax.rsqrt(ms + EPS) * g


def _rope_tables(seq):
    rows = seq // GRID_W
    inv_freq = 1.0 / (ROPE_THETA ** (jnp.arange(0, ROPE_AXIS_DIM, 2, dtype=F32) / ROPE_AXIS_DIM))
    ang_r = jnp.arange(rows, dtype=F32)[:, None] * inv_freq[None, :]
    ang_c = jnp.arange(GRID_W, dtype=F32)[:, None] * inv_freq[None, :]

    def expand_r(a):
        return jnp.repeat(a, GRID_W, axis=0)

    def expand_c(a):
        return jnp.tile(a, (rows, 1))

    cr, sr = expand_r(jnp.cos(ang_r)), expand_r(jnp.sin(ang_r))
    cc, sc = expand_c(jnp.cos(ang_c)), expand_c(jnp.sin(ang_c))
    cos = jnp.concatenate([cr, cr, cc, cc], axis=-1)
    sin = jnp.concatenate([-sr, sr, -sc, sc], axis=-1)
    return cos, sin


def _channel_dft_table():
    c = np.arange(F_CH)
    ang = 2.0 * np.pi * np.outer(c, c) / F_CH
    cs, sn = np.cos(ang) / math.sqrt(F_CH), np.sin(ang) / math.sqrt(F_CH)
    t = np.zeros((F_WIDTH, 2 * F_WIDTH), np.float64)
    for g in range(F_GROUPS):
        sl = slice(g * F_CH, (g + 1) * F_CH)
        t[sl, sl] = cs
        t[sl, F_WIDTH + g * F_CH:F_WIDTH + (g + 1) * F_CH] = -sn
    return jnp.asarray(t, dtype=F32).astype(BF16)


def _stage1_table(n1):
    k = np.arange(n1)
    ang = 2.0 * np.pi * np.outer(k, k) / n1
    cs, sn = np.cos(ang) / math.sqrt(n1), np.sin(ang) / math.sqrt(n1)
    return jnp.asarray(np.block([[cs, sn], [-sn, cs]]), dtype=F32).astype(BF16)


def _stage2_tables(n1):
    n2 = FFT_N2
    n = n1 * n2
    alpha = 2.0 * np.pi * np.outer(np.arange(n1), np.arange(n2)) / n
    beta = 2.0 * np.pi * np.outer(np.arange(n2), np.arange(n2)) / n2
    s = 1.0 / math.sqrt(n2)
    as_f32 = lambda a: jnp.asarray(a, dtype=F32)
    return as_f32(np.cos(alpha)), as_f32(np.sin(alpha)), as_f32(np.cos(beta) * s), as_f32(np.sin(beta) * s)


def _in_proj_kernel(x_ref, gmix_ref, win_ref, dftc_ref, gq_ref, gk_ref, cos_ref, sin_ref,
                    ur_ref, ui_ref, q_ref, k_ref, v_ref, *, q_scale, row_splits):
    tm = x_ref.shape[0]
    sub = tm // row_splits
    lane = lax.broadcasted_iota(jnp.int32, (sub, HEAD_DIM), 1)
    first_half = (lane % ROPE_AXIS_DIM) < (ROPE_AXIS_DIM // 2)
    gq = gq_ref[...] * q_scale
    gk = gk_ref[...]

    for r in range(row_splits):
        rows = slice(r * sub, (r + 1) * sub)
        h = _rms(x_ref[rows, :], gmix_ref[...]).astype(BF16)
        z = jnp.dot(h, win_ref[...], preferred_element_type=F32)

        u = jnp.dot(z[:, :F_WIDTH].astype(BF16), dftc_ref[...], preferred_element_type=F32)
        ur_ref[rows, :] = u[:, :F_WIDTH].astype(BF16)
        ui_ref[rows, :] = u[:, F_WIDTH:].astype(BF16)

        cos = cos_ref[rows, :]
        sin = sin_ref[rows, :]

        def norm_rope(zh, g):
            y = _rms(zh, g)
            partner = jnp.where(first_half,
                                pltpu.roll(y, HEAD_DIM - ROPE_AXIS_DIM // 2, 1),
                                pltpu.roll(y, ROPE_AXIS_DIM // 2, 1))
            return y * cos + partner * sin

        for hh in range(N_Q_HEADS):
            lo = F_WIDTH + hh * HEAD_DIM
            q = norm_rope(z[:, lo:lo + HEAD_DIM], gq)
            q_ref[rows, hh * HEAD_DIM:(hh + 1) * HEAD_DIM] = q.astype(BF16)
        for hh in range(N_KV_HEADS):
            lo = F_WIDTH + Q_WIDTH + hh * HEAD_DIM
            k_ref[rows, hh * HEAD_DIM:(hh + 1) * HEAD_DIM] = norm_rope(z[:, lo:lo + HEAD_DIM], gk).astype(BF16)
            lo = F_WIDTH + Q_WIDTH + KV_WIDTH + hh * HEAD_DIM
            v_ref[rows, hh * HEAD_DIM:(hh + 1) * HEAD_DIM] = z[:, lo:lo + HEAD_DIM].astype(BF16)


def _in_proj(x2d, batch, seq, g_mix, w_in, dftc, g_q, g_k, cos, sin, tm=512):
    t, d = x2d.shape
    nb = seq // tm
    in_w = w_in.shape[1]
    row = lambda i: (i, 0)
    pos = lambda i: (i % nb, 0)
    q_scale = LOG2E / math.sqrt(HEAD_DIM)
    return pl.pallas_call(
        functools.partial(_in_proj_kernel, q_scale=q_scale, row_splits=2),
        grid=(t // tm,),
        in_specs=[
            pl.BlockSpec((tm, d), row),
            _const_spec((1, d)),
            _const_spec((d, in_w)),
            _const_spec((F_WIDTH, 2 * F_WIDTH)),
            _const_spec((1, HEAD_DIM)),
            _const_spec((1, HEAD_DIM)),
            pl.BlockSpec((tm, HEAD_DIM), pos),
            pl.BlockSpec((tm, HEAD_DIM), pos),
        ],
        out_specs=[
            pl.BlockSpec((tm, F_WIDTH), row),
            pl.BlockSpec((tm, F_WIDTH), row),
            pl.BlockSpec((tm, Q_WIDTH), row),
            pl.BlockSpec((tm, KV_WIDTH), row),
            pl.BlockSpec((tm, KV_WIDTH), row),
        ],
        out_shape=[
            jax.ShapeDtypeStruct((t, F_WIDTH), BF16),
            jax.ShapeDtypeStruct((t, F_WIDTH), BF16),
            jax.ShapeDtypeStruct((t, Q_WIDTH), BF16),
            jax.ShapeDtypeStruct((t, KV_WIDTH), BF16),
            jax.ShapeDtypeStruct((t, KV_WIDTH), BF16),
        ],
        compiler_params=_params("parallel"),
        name="in_proj",
    )(x2d, g_mix, w_in, dftc, g_q, g_k, cos, sin)


def _fourier1_kernel(m1_ref, ur_ref, ui_ref, yr_ref, yi_ref):
    n1 = ur_ref.shape[0]
    u = jnp.concatenate([ur_ref[...], ui_ref[...]], axis=0)
    y = jnp.dot(m1_ref[...], u, preferred_element_type=F32)
    yr_ref[...] = y[:n1].astype(BF16)
    yi_ref[...] = y[n1:].astype(BF16)


def _fourier2_kernel(ca_ref, sa_ref, cb_ref, sb_ref, wbd_ref, yr_ref, yi_ref, o_ref, *, kb):
    n2 = FFT_N2
    g = pl.program_id(1)
    cb = cb_ref[...]
    sb = sb_ref[...]
    wbd = wbd_ref[...]
    for j in range(kb):
        k1 = g * kb + j
        ca = ca_ref[pl.ds(k1, 1), :]
        sa = sa_ref[pl.ds(k1, 1), :]
        tw = jnp.concatenate([cb * ca - sb * sa, sb * ca + cb * sa], axis=1).astype(BF16)
        yy = jnp.concatenate([yr_ref[j * n2:(j + 1) * n2, :], yi_ref[j * n2:(j + 1) * n2, :]], axis=0)
        fr = jnp.dot(tw, yy, preferred_element_type=F32)
        o = jnp.dot(fr.astype(BF16), wbd, preferred_element_type=F32)
        o_ref[:, j * F_WIDTH:(j + 1) * F_WIDTH] = o.astype(BF16)


def _fourier(ur, ui, batch, seq, wbd, kb=8):
    n2 = FFT_N2
    n1 = seq // n2
    cols = n2 * F_WIDTH
    tc = min(cols, (4 * 1024 * 1024) // (2 * n1 * 4))
    m1 = _stage1_table(n1)
    ca, sa, cb, sb = _stage2_tables(n1)
    urv = ur.reshape(batch, n1, cols)
    uiv = ui.reshape(batch, n1, cols)
    blk1 = pl.BlockSpec((None, n1, tc), lambda b, c: (b, 0, c))
    yr, yi = pl.pallas_call(
        _fourier1_kernel,
        grid=(batch, cols // tc),
        in_specs=[_const_spec((2 * n1, 2 * n1)), blk1, blk1],
        out_specs=[blk1, blk1],
        out_shape=[jax.ShapeDtypeStruct((batch, n1, cols), BF16)] * 2,
        compiler_params=_params("parallel", "parallel"),
        name="fourier_stage1",
    )(m1, urv, uiv)

    kb = min(kb, n1)
    yblk = pl.BlockSpec((None, kb * n2, F_WIDTH), lambda b, g: (b, g, 0))
    out = pl.pallas_call(
        functools.partial(_fourier2_kernel, kb=kb),
        grid=(batch, n1 // kb),
        in_specs=[_const_spec((n1, n2)), _const_spec((n1, n2)), _const_spec((n2, n2)), _const_spec((n2, n2)),
                  _const_spec((F_WIDTH, F_WIDTH)), yblk, yblk],
        out_specs=pl.BlockSpec((None, n2, kb * F_WIDTH), lambda b, g: (b, 0, g)),
        out_shape=jax.ShapeDtypeStruct((batch, n2, n1 * F_WIDTH), BF16),
        compiler_params=_params("parallel", "parallel"),
        name="fourier_stage2",
    )(ca, sa, cb, sb, wbd, yr.reshape(batch, seq, F_WIDTH), yi.reshape(batch, seq, F_WIDTH))
    return out.reshape(batch * seq, F_WIDTH)


def _scores(k, q_ref, g):
    q = q_ref[:, g * HEAD_DIM:(g + 1) * HEAD_DIM]
    return lax.dot_general(k, q, (((1,), (1,)), ((), ())), preferred_element_type=F32)


def _attention_kernel(q_ref, k_ref, v_ref, o_ref, m_ref, l_ref, acc_ref):
    j = pl.program_id(3)

    @pl.when(j == 0)
    def _():
        m_ref[...] = jnp.full(m_ref.shape, -jnp.inf, F32)
        l_ref[...] = jnp.zeros(l_ref.shape, F32)
        acc_ref[...] = jnp.zeros(acc_ref.shape, F32)

    k = k_ref[...]
    vt = v_ref[...].T
    for g in range(Q_PER_KV):
        s = _scores(k, q_ref, g)
        m_prev = m_ref[g]
        m_new = jnp.maximum(m_prev, jnp.max(s, axis=0, keepdims=True))
        alpha = jnp.exp2(m_prev - m_new)
        p = jnp.exp2(s - m_new)
        l_ref[g] = alpha * l_ref[g] + jnp.sum(p, axis=0, keepdims=True)
        acc_ref[g] = alpha * acc_ref[g] + jnp.dot(vt, p.astype(BF16), preferred_element_type=F32)
        m_ref[g] = m_new

    @pl.when(j == pl.num_programs(3) - 1)
    def _():
        for g in range(Q_PER_KV):
            o = acc_ref[g] * (1.0 / l_ref[g])
            o_ref[:, g * HEAD_DIM:(g + 1) * HEAD_DIM] = o.T.astype(BF16)


def _attention_bounded_kernel(q_ref, k_ref, v_ref, o_ref, l_ref, acc_ref):
    j = pl.program_id(3)

    @pl.when(j == 0)
    def _():
        l_ref[...] = jnp.zeros(l_ref.shape, F32)
        acc_ref[...] = jnp.zeros(acc_ref.shape, F32)

    k = k_ref[...]
    vt = v_ref[...].T
    tk = k.shape[0]
    for g in range(Q_PER_KV):
        p = jnp.exp2(_scores(k, q_ref, g))
        l_ref[g] += jnp.sum(p.reshape(tk // SUBLANES, SUBLANES, p.shape[1]), axis=0)
        acc_ref[g] += jnp.dot(vt, p.astype(BF16), preferred_element_type=F32)

    @pl.when(j == pl.num_programs(3) - 1)
    def _():
        for g in range(Q_PER_KV):
            denom = jnp.sum(l_ref[g], axis=0, keepdims=True)
            o = acc_ref[g] * (1.0 / denom)
            o_ref[:, g * HEAD_DIM:(g + 1) * HEAD_DIM] = o.T.astype(BF16)


def _attention_call(body, scratch, q, k, v, tq, tk, name):
    batch, seq, _ = q.shape
    tq, tk = min(tq, seq), min(tk, seq)
    gw = Q_PER_KV * HEAD_DIM
    return pl.pallas_call(
        body,
        grid=(batch, N_KV_HEADS, seq // tq, seq // tk),
        in_specs=[
            pl.BlockSpec((None, tq, gw), lambda b, h, i, j: (b, i, h)),
            pl.BlockSpec((None, tk, HEAD_DIM), lambda b, h, i, j: (b, j, h)),
            pl.BlockSpec((None, tk, HEAD_DIM), lambda b, h, i, j: (b, j, h)),
        ],
        out_specs=pl.BlockSpec((None, tq, gw), lambda b, h, i, j: (b, i, h)),
        out_shape=jax.ShapeDtypeStruct((batch, seq, Q_WIDTH), BF16),
        scratch_shapes=scratch(tq),
        compiler_params=_params("parallel", "parallel", "parallel", "arbitrary"),
        name=name,
    )(q, k, v)


def _attention(q, k, v, score_bound):
    online_scratch = lambda tq: [pltpu.VMEM((Q_PER_KV, 1, tq), F32), pltpu.VMEM((Q_PER_KV, 1, tq), F32),
                                 pltpu.VMEM((Q_PER_KV, HEAD_DIM, tq), F32)]
    bounded_scratch = lambda tq: [pltpu.VMEM((Q_PER_KV, SUBLANES, tq), F32),
                                  pltpu.VMEM((Q_PER_KV, HEAD_DIM, tq), F32)]
    seq = q.shape[1]
    tk = min(BOUNDED_KEY_TILE, seq)
    tq = min(seq, max(BOUNDED_SCORE_TILE // tk, 2 * HEAD_DIM))
    bounded = functools.partial(_attention_call, _attention_bounded_kernel, bounded_scratch,
                                tq=tq, tk=tk, name="self_attention_bounded")
    online = functools.partial(_attention_call, _attention_kernel, online_scratch,
                               tq=512, tk=512, name="self_attention_online")
    return lax.cond(score_bound <= SAFE_SCORE_BOUND, bounded, online, q, k, v)


def _mem_kv_kernel(mem_ref, gmem_ref, wckv_ref, gck_ref, kt_ref, v_ref):
    d = mem_ref.shape[1]
    xh = d // X_HEADS
    h = _rms(mem_ref[...], gmem_ref[...]).astype(BF16)
    kv = jnp.dot(h, wckv_ref[...], preferred_element_type=F32)
    gck = gck_ref[...]
    for hh in range(X_HEADS):
        kh = _rms(kv[:, hh * xh:(hh + 1) * xh], gck)
        kt_ref[hh * xh:(hh + 1) * xh, :] = kh.T.astype(BF16)
    v_ref[...] = kv[:, d:].astype(BF16)


def _mem_kv(mem, g_mem, w_ckv, g_ck):
    batch, m, d = mem.shape
    return pl.pallas_call(
        _mem_kv_kernel,
        grid=(batch,),
        in_specs=[pl.BlockSpec((None, m, d), lambda b: (b, 0, 0)),
                  _const_spec((1, d)), _const_spec((d, 2 * d)), _const_spec((1, d // X_HEADS))],
        out_specs=[pl.BlockSpec((None, d, m), lambda b: (b, 0, 0)),
                   pl.BlockSpec((None, m, d), lambda b: (b, 0, 0))],
        out_shape=[jax.ShapeDtypeStruct((batch, d, m), BF16), jax.ShapeDtypeStruct((batch, m, d), BF16)],
        compiler_params=_params("parallel"),
        name="mem_kv",
    )(mem, g_mem, w_ckv, g_ck)


def _post_kernel(x_ref, f_ref, a_ref, wout_ref, gcross_ref, wcq_ref, gcq_ref, kt_ref, v_ref, wco_ref,
                 gmlp_ref, wup_ref, wdown_ref, y_ref, *, x_scale, ff_chunk):
    d = x_ref.shape[1]
    xh = d // X_HEADS
    x1 = (x_ref[...]
          + jnp.dot(f_ref[...], wout_ref[:F_WIDTH, :], preferred_element_type=F32)
          + jnp.dot(a_ref[...], wout_ref[F_WIDTH:, :], preferred_element_type=F32))

    qc = jnp.dot(_rms(x1, gcross_ref[...]).astype(BF16), wcq_ref[...], preferred_element_type=F32)
    gcq = gcq_ref[...]
    heads = []
    for hh in range(X_HEADS):
        sl = slice(hh * xh, (hh + 1) * xh)
        qh = (_rms(qc[:, sl], gcq) * x_scale).astype(BF16)
        s = jnp.dot(qh, kt_ref[sl, :], preferred_element_type=F32)
        p = jnp.exp2(s - jnp.max(s, axis=-1, keepdims=True))
        o = jnp.dot(p.astype(BF16), v_ref[:, sl], preferred_element_type=F32)
        heads.append((o * (1.0 / jnp.sum(p, axis=-1, keepdims=True))).astype(BF16))
    x2 = x1 + jnp.dot(jnp.concatenate(heads, axis=-1), wco_ref[...], preferred_element_type=F32)

    h3 = _rms(x2, gmlp_ref[...]).astype(BF16)
    acc = x2
    for c in range(wup_ref.shape[1] // ff_chunk):
        sl = slice(c * ff_chunk, (c + 1) * ff_chunk)
        u = jnp.maximum(jnp.dot(h3, wup_ref[:, sl], preferred_element_type=F32), 0.0)
        acc = acc + jnp.dot((u * u).astype(BF16), wdown_ref[sl, :], preferred_element_type=F32)
    y_ref[...] = acc


def _post(x2d, f_out, a_out, kt, v, seq, w_out, g_cross, w_cq, g_cq, w_co, g_mlp, w_up, w_down, tm=512):
    t, d = x2d.shape
    nb = seq // tm
    m = v.shape[1]
    d_ff = w_up.shape[1]
    row = lambda i: (i, 0)
    x_scale = LOG2E / math.sqrt(d // X_HEADS)
    return pl.pallas_call(
        functools.partial(_post_kernel, x_scale=x_scale, ff_chunk=1024),
        grid=(t // tm,),
        in_specs=[
            pl.BlockSpec((tm, d), row),
            pl.BlockSpec((tm, F_WIDTH), row),
            pl.BlockSpec((tm, Q_WIDTH), row),
            _const_spec((d, d)),
            _const_spec((1, d)),
            _const_spec((d, d)),
            _const_spec((1, d // X_HEADS)),
            pl.BlockSpec((None, d, m), lambda i: (i // nb, 0, 0)),
            pl.BlockSpec((None, m, d), lambda i: (i // nb, 0, 0)),
            _const_spec((d, d)),
            _const_spec((1, d)),
            _const_spec((d, d_ff)),
            _const_spec((d_ff, d)),
        ],
        out_specs=pl.BlockSpec((tm, d), row),
        out_shape=jax.ShapeDtypeStruct((t, d), F32),
        compiler_params=_params("parallel"),
        name="post",
    )(x2d, f_out, a_out, w_out, g_cross, w_cq, g_cq, kt, v, w_co, g_mlp, w_up, w_down)


def _block_diag_weights(w_f):
    out = jnp.zeros((F_WIDTH, F_WIDTH), w_f.dtype)
    for g in range(F_GROUPS):
        out = out.at[g * F_CH:(g + 1) * F_CH, g * F_CH:(g + 1) * F_CH].set(w_f[g])
    return out


def _layer(x, mem, cos, sin, dftc, p):
    batch, seq, d = x.shape
    x2d = x.reshape(batch * seq, d)
    ur, ui, q, k, v = _in_proj(x2d, batch, seq, p["g_mix"], p["w_in"], dftc, p["g_q"], p["g_k"],
                                cos[:seq], sin[:seq])
    f_out = _fourier(ur, ui, batch, seq, p["wbd"])
    score_bound = (HEAD_DIM * LOG2E / math.sqrt(HEAD_DIM)) * jnp.max(jnp.abs(p["g_q"])) * jnp.max(jnp.abs(p["g_k"]))
    a_out = _attention(q.reshape(batch, seq, Q_WIDTH), k.reshape(batch, seq, KV_WIDTH),
                       v.reshape(batch, seq, KV_WIDTH),
                       score_bound).reshape(batch * seq, Q_WIDTH)
    kt, v = _mem_kv(mem, p["g_mem"], p["w_ckv"], p["g_ck"])
    y = _post(x2d, f_out, a_out, kt, v, seq, p["w_out"], p["g_cross"], p["w_cq"], p["g_cq"], p["w_co"],
              p["g_mlp"], p["w_up"], p["w_down"])
    return y.reshape(batch, seq, d)


def kernel(x_prompt, x_sample, mem_prompt, mem_sample, g_mix, w_in, w_fourier, g_q, g_k, w_out, g_cross, g_mem,
           w_cq, w_ckv, g_cq, g_ck, w_co, g_mlp, w_up, w_down):
    depth = w_in.shape[0]
    max_seq = max(x_prompt.shape[1], x_sample.shape[1])
    cos, sin = _rope_tables(max_seq)
    dftc = _channel_dft_table()
    yp, ys = x_prompt, x_sample
    for l in range(depth):
        row = lambda g: g[l][None, :]
        p = dict(
            g_mix=row(g_mix), w_in=w_in[l].astype(BF16), wbd=_block_diag_weights(w_fourier[l]).astype(BF16),
            g_q=row(g_q), g_k=row(g_k), w_out=w_out[l].astype(BF16), g_cross=row(g_cross), g_mem=row(g_mem),
            w_cq=w_cq[l].astype(BF16), w_ckv=w_ckv[l].astype(BF16), g_cq=row(g_cq), g_ck=row(g_ck),
            w_co=w_co[l].astype(BF16), g_mlp=row(g_mlp), w_up=w_up[l].astype(BF16), w_down=w_down[l].astype(BF16),
        )
        yp = _layer(yp, mem_prompt, cos, sin, dftc, p)
        ys = _layer(ys, mem_sample, cos, sin, dftc, p)
    return (yp, ys)
```

```python
import functools
import math

import numpy as np
import jax
import jax.numpy as jnp
from jax import lax
from jax.experimental import pallas as pl
from jax.experimental.pallas import tpu as pltpu

F32 = jnp.float32
BF16 = jnp.bfloat16

GRID_W = 64
F_GROUPS = 4
F_CH = 64
F_WIDTH = F_GROUPS * F_CH
HEAD_DIM = 128
N_Q_HEADS = 6
N_KV_HEADS = 2
Q_PER_KV = N_Q_HEADS // N_KV_HEADS
Q_WIDTH = N_Q_HEADS * HEAD_DIM
KV_WIDTH = N_KV_HEADS * HEAD_DIM
ROPE_AXIS_DIM = HEAD_DIM // 2
ROPE_THETA = 10000.0
X_HEADS = 4
EPS = 1e-6
LOG2E = 1.4426950408889634

VMEM_LIMIT_BYTES = 56 * 1024 * 1024
FFT_N2 = 128
SAFE_SCORE_BOUND = 60.0
SUBLANES = 8
BOUNDED_KEY_TILE = 16384
BOUNDED_SCORE_TILE = 8 * 1024 * 1024


def _params(*sem):
    return pltpu.CompilerParams(dimension_semantics=sem, vmem_limit_bytes=VMEM_LIMIT_BYTES)


def _const_spec(shape):
    nd = len(shape)
    return pl.BlockSpec(shape, lambda *_: (0,) * nd, pipeline_mode=pl.Buffered(1))


def _rms(x, g):
    ms = jnp.mean(x * x, axis=-1, keepdims=True)
    return x * lax.rsqrt(ms + EPS) * g


def _rope_tables(seq):
    rows = seq // GRID_W
    inv_freq = 1.0 / (ROPE_THETA ** (jnp.arange(0, ROPE_AXIS_DIM, 2, dtype=F32) / ROPE_AXIS_DIM))
    ang_r = jnp.arange(rows, dtype=F32)[:, None] * inv_freq[None, :]
    ang_c = jnp.arange(GRID_W, dtype=F32)[:, None] * inv_freq[None, :]

    def expand_r(a):
        return jnp.repeat(a, GRID_W, axis=0)

    def expand_c(a):
        return jnp.tile(a, (rows, 1))

    cr, sr = expand_r(jnp.cos(ang_r)), expand_r(jnp.sin(ang_r))
    cc, sc = expand_c(jnp.cos(ang_c)), expand_c(jnp.sin(ang_c))
    cos = jnp.concatenate([cr, cr, cc, cc], axis=-1)
    sin = jnp.concatenate([-sr, sr, -sc, sc], axis=-1)
    return cos, sin


def _channel_dft_table():
    c = np.arange(F_CH)
    ang = 2.0 * np.pi * np.outer(c, c) / F_CH
    cs, sn = np.cos(ang) / math.sqrt(F_CH), np.sin(ang) / math.sqrt(F_CH)
    t = np.zeros((F_WIDTH, 2 * F_WIDTH), np.float64)
    for g in range(F_GROUPS):
        sl = slice(g * F_CH, (g + 1) * F_CH)
        t[sl, sl] = cs
        t[sl, F_WIDTH + g * F_CH:F_WIDTH + (g + 1) * F_CH] = -sn
    return jnp.asarray(t, dtype=F32).astype(BF16)


def _stage1_table(n1):
    k = np.arange(n1)
    ang = 2.0 * np.pi * np.outer(k, k) / n1
    cs, sn = np.cos(ang) / math.sqrt(n1), np.sin(ang) / math.sqrt(n1)
    return jnp.asarray(np.block([[cs, sn], [-sn, cs]]), dtype=F32).astype(BF16)


def _stage2_tables(n1):
    n2 = FFT_N2
    n = n1 * n2
    alpha = 2.0 * np.pi * np.outer(np.arange(n1), np.arange(n2)) / n
    beta = 2.0 * np.pi * np.outer(np.arange(n2), np.arange(n2)) / n2
    s = 1.0 / math.sqrt(n2)
    as_f32 = lambda a: jnp.asarray(a, dtype=F32)
    return as_f32(np.cos(alpha)), as_f32(np.sin(alpha)), as_f32(np.cos(beta) * s), as_f32(np.sin(beta) * s)


def _in_proj_kernel(x_ref, gmix_ref, win_ref, dftc_ref, gq_ref, gk_ref, cos_ref, sin_ref,
                    ur_ref, ui_ref, q_ref, k_ref, v_ref, *, q_scale, row_splits):
    tm = x_ref.shape[0]
    sub = tm // row_splits
    lane = lax.broadcasted_iota(jnp.int32, (sub, HEAD_DIM), 1)
    first_half = (lane % ROPE_AXIS_DIM) < (ROPE_AXIS_DIM // 2)
    gq = gq_ref[...] * q_scale
    gk = gk_ref[...]

    for r in range(row_splits):
        rows = slice(r * sub, (r + 1) * sub)
        h = _rms(x_ref[rows, :], gmix_ref[...]).astype(BF16)
        z = jnp.dot(h, win_ref[...], preferred_element_type=F32)

        u = jnp.dot(z[:, :F_WIDTH].astype(BF16), dftc_ref[...], preferred_element_type=F32)
        ur_ref[rows, :] = u[:, :F_WIDTH].astype(BF16)
        ui_ref[rows, :] = u[:, F_WIDTH:].astype(BF16)

        cos = cos_ref[rows, :]
        sin = sin_ref[rows, :]

        def norm_rope(zh, g):
            y = _rms(zh, g)
            partner = jnp.where(first_half,
                                pltpu.roll(y, HEAD_DIM - ROPE_AXIS_DIM // 2, 1),
                                pltpu.roll(y, ROPE_AXIS_DIM // 2, 1))
            return y * cos + partner * sin

        for hh in range(N_Q_HEADS):
            lo = F_WIDTH + hh * HEAD_DIM
            q = norm_rope(z[:, lo:lo + HEAD_DIM], gq)
            q_ref[rows, hh * HEAD_DIM:(hh + 1) * HEAD_DIM] = q.astype(BF16)
        for hh in range(N_KV_HEADS):
            lo = F_WIDTH + Q_WIDTH + hh * HEAD_DIM
            k_ref[rows, hh * HEAD_DIM:(hh + 1) * HEAD_DIM] = norm_rope(z[:, lo:lo + HEAD_DIM], gk).astype(BF16)
            lo = F_WIDTH + Q_WIDTH + KV_WIDTH + hh * HEAD_DIM
            v_ref[rows, hh * HEAD_DIM:(hh + 1) * HEAD_DIM] = z[:, lo:lo + HEAD_DIM].astype(BF16)


def _in_proj(x2d, batch, seq, g_mix, w_in, dftc, g_q, g_k, cos, sin, tm=512):
    t, d = x2d.shape
    nb = seq // tm
    in_w = w_in.shape[1]
    row = lambda i: (i, 0)
    pos = lambda i: (i % nb, 0)
    q_scale = LOG2E / math.sqrt(HEAD_DIM)
    return pl.pallas_call(
        functools.partial(_in_proj_kernel, q_scale=q_scale, row_splits=2),
        grid=(t // tm,),
        in_specs=[
            pl.BlockSpec((tm, d), row),
            _const_spec((1, d)),
            _const_spec((d, in_w)),
            _const_spec((F_WIDTH, 2 * F_WIDTH)),
            _const_spec((1, HEAD_DIM)),
            _const_spec((1, HEAD_DIM)),
            pl.BlockSpec((tm, HEAD_DIM), pos),
            pl.BlockSpec((tm, HEAD_DIM), pos),
        ],
        out_specs=[
            pl.BlockSpec((tm, F_WIDTH), row),
            pl.BlockSpec((tm, F_WIDTH), row),
            pl.BlockSpec((tm, Q_WIDTH), row),
            pl.BlockSpec((tm, KV_WIDTH), row),
            pl.BlockSpec((tm, KV_WIDTH), row),
        ],
        out_shape=[
            jax.ShapeDtypeStruct((t, F_WIDTH), BF16),
            jax.ShapeDtypeStruct((t, F_WIDTH), BF16),
            jax.ShapeDtypeStruct((t, Q_WIDTH), BF16),
            jax.ShapeDtypeStruct((t, KV_WIDTH), BF16),
            jax.ShapeDtypeStruct((t, KV_WIDTH), BF16),
        ],
        compiler_params=_params("parallel"),
        name="in_proj",
    )(x2d, g_mix, w_in, dftc, g_q, g_k, cos, sin)


def _fourier1_kernel(m1_ref, ur_ref, ui_ref, yr_ref, yi_ref):
    n1 = ur_ref.shape[0]
    u = jnp.concatenate([ur_ref[...], ui_ref[...]], axis=0)
    y = jnp.dot(m1_ref[...], u, preferred_element_type=F32)
    yr_ref[...] = y[:n1].astype(BF16)
    yi_ref[...] = y[n1:].astype(BF16)


def _fourier2_kernel(ca_ref, sa_ref, cb_ref, sb_ref, wbd_ref, yr_ref, yi_ref, o_ref, *, kb):
    n2 = FFT_N2
    g = pl.program_id(1)
    cb = cb_ref[...]
    sb = sb_ref[...]
    wbd = wbd_ref[...]
    for j in range(kb):
        k1 = g * kb + j
        ca = ca_ref[pl.ds(k1, 1), :]
        sa = sa_ref[pl.ds(k1, 1), :]
        tw = jnp.concatenate([cb * ca - sb * sa, sb * ca + cb * sa], axis=1).astype(BF16)
        yy = jnp.concatenate([yr_ref[j * n2:(j + 1) * n2, :], yi_ref[j * n2:(j + 1) * n2, :]], axis=0)
        fr = jnp.dot(tw, yy, preferred_element_type=F32)
        o = jnp.dot(fr.astype(BF16), wbd, preferred_element_type=F32)
        o_ref[:, j * F_WIDTH:(j + 1) * F_WIDTH] = o.astype(BF16)


def _fourier(ur, ui, batch, seq, wbd, kb=8):
    n2 = FFT_N2
    n1 = seq // n2
    cols = n2 * F_WIDTH
    tc = min(cols, (4 * 1024 * 1024) // (2 * n1 * 4))
    m1 = _stage1_table(n1)
    ca, sa, cb, sb = _stage2_tables(n1)
    urv = ur.reshape(batch, n1, cols)
    uiv = ui.reshape(batch, n1, cols)
    blk1 = pl.BlockSpec((None, n1, tc), lambda b, c: (b, 0, c))
    yr, yi = pl.pallas_call(
        _fourier1_kernel,
        grid=(batch, cols // tc),
        in_specs=[_const_spec((2 * n1, 2 * n1)), blk1, blk1],
        out_specs=[blk1, blk1],
        out_shape=[jax.ShapeDtypeStruct((batch, n1, cols), BF16)] * 2,
        compiler_params=_params("parallel", "parallel"),
        name="fourier_stage1",
    )(m1, urv, uiv)

    kb = min(kb, n1)
    yblk = pl.BlockSpec((None, kb * n2, F_WIDTH), lambda b, g: (b, g, 0))
    out = pl.pallas_call(
        functools.partial(_fourier2_kernel, kb=kb),
        grid=(batch, n1 // kb),
        in_specs=[_const_spec((n1, n2)), _const_spec((n1, n2)), _const_spec((n2, n2)), _const_spec((n2, n2)),
                  _const_spec((F_WIDTH, F_WIDTH)), yblk, yblk],
        out_specs=pl.BlockSpec((None, n2, kb * F_WIDTH), lambda b, g: (b, 0, g)),
        out_shape=jax.ShapeDtypeStruct((batch, n2, n1 * F_WIDTH), BF16),
        compiler_params=_params("parallel", "parallel"),
        name="fourier_stage2",
    )(ca, sa, cb, sb, wbd, yr.reshape(batch, seq, F_WIDTH), yi.reshape(batch, seq, F_WIDTH))
    return out.reshape(batch * seq, F_WIDTH)


def _scores(k, q_ref, g):
    q = q_ref[:, g * HEAD_DIM:(g + 1) * HEAD_DIM]
    return lax.dot_general(k, q, (((1,), (1,)), ((), ())), preferred_element_type=F32)


def _attention_kernel(q_ref, k_ref, v_ref, o_ref, m_ref, l_ref, acc_ref):
    j = pl.program_id(3)

    @pl.when(j == 0)
    def _():
        m_ref[...] = jnp.full(m_ref.shape, -jnp.inf, F32)
        l_ref[...] = jnp.zeros(l_ref.shape, F32)
        acc_ref[...] = jnp.zeros(acc_ref.shape, F32)

    k = k_ref[...]
    vt = v_ref[...].T
    for g in range(Q_PER_KV):
        s = _scores(k, q_ref, g)
        m_prev = m_ref[g]
        m_new = jnp.maximum(m_prev, jnp.max(s, axis=0, keepdims=True))
        alpha = jnp.exp2(m_prev - m_new)
        p = jnp.exp2(s - m_new)
        l_ref[g] = alpha * l_ref[g] + jnp.sum(p, axis=0, keepdims=True)
        acc_ref[g] = alpha * acc_ref[g] + jnp.dot(vt, p.astype(BF16), preferred_element_type=F32)
        m_ref[g] = m_new

    @pl.when(j == pl.num_programs(3) - 1)
    def _():
        for g in range(Q_PER_KV):
            o = acc_ref[g] * (1.0 / l_ref[g])
            o_ref[:, g * HEAD_DIM:(g + 1) * HEAD_DIM] = o.T.astype(BF16)


def _attention_bounded_kernel(q_ref, k_ref, v_ref, o_ref, l_ref, acc_ref):
    j = pl.program_id(3)

    @pl.when(j == 0)
    def _():
        l_ref[...] = jnp.zeros(l_ref.shape, F32)
        acc_ref[...] = jnp.zeros(acc_ref.shape, F32)

    k = k_ref[...]
    vt = v_ref[...].T
    tk = k.shape[0]
    for g in range(Q_PER_KV):
        p = jnp.exp2(_scores(k, q_ref, g))
        l_ref[g] += jnp.sum(p.reshape(tk // SUBLANES, SUBLANES, p.shape[1]), axis=0)
        acc_ref[g] += jnp.dot(vt, p.astype(BF16), preferred_element_type=F32)

    @pl.when(j == pl.num_programs(3) - 1)
    def _():
        for g in range(Q_PER_KV):
            denom = jnp.sum(l_ref[g], axis=0, keepdims=True)
            o = acc_ref[g] * (1.0 / denom)
            o_ref[:, g * HEAD_DIM:(g + 1) * HEAD_DIM] = o.T.astype(BF16)


def _attention_call(body, scratch, q, k, v, tq, tk, name):
    batch, seq, _ = q.shape
    tq, tk = min(tq, seq), min(tk, seq)
    gw = Q_PER_KV * HEAD_DIM
    return pl.pallas_call(
        body,
        grid=(batch, N_KV_HEADS, seq // tq, seq // tk),
        in_specs=[
            pl.BlockSpec((None, tq, gw), lambda b, h, i, j: (b, i, h)),
            pl.BlockSpec((None, tk, HEAD_DIM), lambda b, h, i, j: (b, j, h)),
            pl.BlockSpec((None, tk, HEAD_DIM), lambda b, h, i, j: (b, j, h)),
        ],
        out_specs=pl.BlockSpec((None, tq, gw), lambda b, h, i, j: (b, i, h)),
        out_shape=jax.ShapeDtypeStruct((batch, seq, Q_WIDTH), BF16),
        scratch_shapes=scratch(tq),
        compiler_params=_params("parallel", "parallel", "parallel", "arbitrary"),
        name=name,
    )(q, k, v)


def _attention(q, k, v, score_bound):
    online_scratch = lambda tq: [pltpu.VMEM((Q_PER_KV, 1, tq), F32), pltpu.VMEM((Q_PER_KV, 1, tq), F32),
                                 pltpu.VMEM((Q_PER_KV, HEAD_DIM, tq), F32)]
    bounded_scratch = lambda tq: [pltpu.VMEM((Q_PER_KV, SUBLANES, tq), F32),
                                  pltpu.VMEM((Q_PER_KV, HEAD_DIM, tq), F32)]
    seq = q.shape[1]
    tk = min(BOUNDED_KEY_TILE, seq)
    tq = min(seq, max(BOUNDED_SCORE_TILE // tk, 2 * HEAD_DIM))
    bounded = functools.partial(_attention_call, _attention_bounded_kernel, bounded_scratch,
                                tq=tq, tk=tk, name="self_attention_bounded")
    online = functools.partial(_attention_call, _attention_kernel, online_scratch,
                               tq=512, tk=512, name="self_attention_online")
    return lax.cond(score_bound <= SAFE_SCORE_BOUND, bounded, online, q, k, v)


def _mem_kv_kernel(mem_ref, gmem_ref, wckv_ref, gck_ref, kt_ref, v_ref):
    d = mem_ref.shape[1]
    xh = d // X_HEADS
    h = _rms(mem_ref[...], gmem_ref[...]).astype(BF16)
    kv = jnp.dot(h, wckv_ref[...], preferred_element_type=F32)
    gck = gck_ref[...]
    for hh in range(X_HEADS):
        kh = _rms(kv[:, hh * xh:(hh + 1) * xh], gck)
        kt_ref[hh * xh:(hh + 1) * xh, :] = kh.T.astype(BF16)
    v_ref[...] = kv[:, d:].astype(BF16)


def _mem_kv(mem, g_mem, w_ckv, g_ck):
    batch, m, d = mem.shape
    return pl.pallas_call(
        _mem_kv_kernel,
        grid=(batch,),
        in_specs=[pl.BlockSpec((None, m, d), lambda b: (b, 0, 0)),
                  _const_spec((1, d)), _const_spec((d, 2 * d)), _const_spec((1, d // X_HEADS))],
        out_specs=[pl.BlockSpec((None, d, m), lambda b: (b, 0, 0)),
                   pl.BlockSpec((None, m, d), lambda b: (b, 0, 0))],
        out_shape=[jax.ShapeDtypeStruct((batch, d, m), BF16), jax.ShapeDtypeStruct((batch, m, d), BF16)],
        compiler_params=_params("parallel"),
        name="mem_kv",
    )(mem, g_mem, w_ckv, g_ck)


def _post_kernel(x_ref, f_ref, a_ref, wout_ref, gcross_ref, wcq_ref, gcq_ref, kt_ref, v_ref, wco_ref,
                 gmlp_ref, wup_ref, wdown_ref, y_ref, *, x_scale, ff_chunk):
    d = x_ref.shape[1]
    xh = d // X_HEADS
    x1 = (x_ref[...]
          + jnp.dot(f_ref[...], wout_ref[:F_WIDTH, :], preferred_element_type=F32)
          + jnp.dot(a_ref[...], wout_ref[F_WIDTH:, :], preferred_element_type=F32))

    qc = jnp.dot(_rms(x1, gcross_ref[...]).astype(BF16), wcq_ref[...], preferred_element_type=F32)
    gcq = gcq_ref[...]
    heads = []
    for hh in range(X_HEADS):
        sl = slice(hh * xh, (hh + 1) * xh)
        qh = (_rms(qc[:, sl], gcq) * x_scale).astype(BF16)
        s = jnp.dot(qh, kt_ref[sl, :], preferred_element_type=F32)
        p = jnp.exp2(s - jnp.max(s, axis=-1, keepdims=True))
        o = jnp.dot(p.astype(BF16), v_ref[:, sl], preferred_element_type=F32)
        heads.append((o * (1.0 / jnp.sum(p, axis=-1, keepdims=True))).astype(BF16))
    x2 = x1 + jnp.dot(jnp.concatenate(heads, axis=-1), wco_ref[...], preferred_element_type=F32)

    h3 = _rms(x2, gmlp_ref[...]).astype(BF16)
    acc = x2
    for c in range(wup_ref.shape[1] // ff_chunk):
        sl = slice(c * ff_chunk, (c + 1) * ff_chunk)
        u = jnp.maximum(jnp.dot(h3, wup_ref[:, sl], preferred_element_type=F32), 0.0)
        acc = acc + jnp.dot((u * u).astype(BF16), wdown_ref[sl, :], preferred_element_type=F32)
    y_ref[...] = acc


def _post(x2d, f_out, a_out, kt, v, seq, w_out, g_cross, w_cq, g_cq, w_co, g_mlp, w_up, w_down, tm=512):
    t, d = x2d.shape
    nb = seq // tm
    m = v.shape[1]
    d_ff = w_up.shape[1]
    row = lambda i: (i, 0)
    x_scale = LOG2E / math.sqrt(d // X_HEADS)
    return pl.pallas_call(
        functools.partial(_post_kernel, x_scale=x_scale, ff_chunk=1024),
        grid=(t // tm,),
        in_specs=[
            pl.BlockSpec((tm, d), row),
            pl.BlockSpec((tm, F_WIDTH), row),
            pl.BlockSpec((tm, Q_WIDTH), row),
            _const_spec((d, d)),
            _const_spec((1, d)),
            _const_spec((d, d)),
            _const_spec((1, d // X_HEADS)),
            pl.BlockSpec((None, d, m), lambda i: (i // nb, 0, 0)),
            pl.BlockSpec((None, m, d), lambda i: (i // nb, 0, 0)),
            _const_spec((d, d)),
            _const_spec((1, d)),
            _const_spec((d, d_ff)),
            _const_spec((d_ff, d)),
        ],
        out_specs=pl.BlockSpec((tm, d), row),
        out_shape=jax.ShapeDtypeStruct((t, d), F32),
        compiler_params=_params("parallel"),
        name="post",
    )(x2d, f_out, a_out, w_out, g_cross, w_cq, g_cq, kt, v, w_co, g_mlp, w_up, w_down)


def _block_diag_weights(w_f):
    out = jnp.zeros((F_WIDTH, F_WIDTH), w_f.dtype)
    for g in range(F_GROUPS):
        out = out.at[g * F_CH:(g + 1) * F_CH, g * F_CH:(g + 1) * F_CH].set(w_f[g])
    return out


def _layer(x, mem, cos, sin, dftc, p):
    batch, seq, d = x.shape
    x2d = x.reshape(batch * seq, d)
    ur, ui, q, k, v = _in_proj(x2d, batch, seq, p["g_mix"], p["w_in"], dftc, p["g_q"], p["g_k"],
                                cos[:seq], sin[:seq])
    f_out = _fourier(ur, ui, batch, seq, p["wbd"])
    score_bound = (HEAD_DIM * LOG2E / math.sqrt(HEAD_DIM)) * jnp.max(jnp.abs(p["g_q"])) * jnp.max(jnp.abs(p["g_k"]))
    a_out = _attention(q.reshape(batch, seq, Q_WIDTH), k.reshape(batch, seq, KV_WIDTH),
                       v.reshape(batch, seq, KV_WIDTH),
                       score_bound).reshape(batch * seq, Q_WIDTH)
    kt, v = _mem_kv(mem, p["g_mem"], p["w_ckv"], p["g_ck"])
    y = _post(x2d, f_out, a_out, kt, v, seq, p["w_out"], p["g_cross"], p["w_cq"], p["g_cq"], p["w_co"],
              p["g_mlp"], p["w_up"], p["w_down"])
    return y.reshape(batch, seq, d)


def kernel(x_prompt, x_sample, mem_prompt, mem_sample, g_mix, w_in, w_fourier, g_q, g_k, w_out, g_cross, g_mem,
           w_cq, w_ckv, g_cq, g_ck, w_co, g_mlp, w_up, w_down):
    depth = w_in.shape[0]
    max_seq = max(x_prompt.shape[1], x_sample.shape[1])
    cos, sin = _rope_tables(max_seq)
    dftc = _channel_dft_table()
    yp, ys = x_prompt, x_sample
    for l in range(depth):
        row = lambda g: g[l][None, :]
        p = dict(
            g_mix=row(g_mix), w_in=w_in[l].astype(BF16), wbd=_block_diag_weights(w_fourier[l]).astype(BF16),
            g_q=row(g_q), g_k=row(g_k), w_out=w_out[l].astype(BF16), g_cross=row(g_cross), g_mem=row(g_mem),
            w_cq=w_cq[l].astype(BF16), w_ckv=w_ckv[l].astype(BF16), g_cq=row(g_cq), g_ck=row(g_ck),
            w_co=w_co[l].astype(BF16), g_mlp=row(g_mlp), w_up=w_up[l].astype(BF16), w_down=w_down[l].astype(BF16),
        )
        yp = _layer(yp, mem_prompt, cos, sin, dftc, p)
        ys = _layer(ys, mem_sample, cos, sin, dftc, p)
    return (yp, ys)
```

```python
import functools
import math

import numpy as np
import jax
import jax.numpy as jnp
from jax import lax
from jax.experimental import pallas as pl
from jax.experimental.pallas import tpu as pltpu

F32 = jnp.float32
BF16 = jnp.bfloat16

GRID_W = 64
F_GROUPS = 4
F_CH = 64
F_WIDTH = F_GROUPS * F_CH
HEAD_DIM = 128
N_Q_HEADS = 6
N_KV_HEADS = 2
Q_PER_KV = N_Q_HEADS // N_KV_HEADS
Q_WIDTH = N_Q_HEADS * HEAD_DIM
KV_WIDTH = N_KV_HEADS * HEAD_DIM
ROPE_AXIS_DIM = HEAD_DIM // 2
ROPE_THETA = 10000.0
X_HEADS = 4
EPS = 1e-6
LOG2E = 1.4426950408889634

VMEM_LIMIT_BYTES = 56 * 1024 * 1024
FFT_N2 = 128
SAFE_SCORE_BOUND = 60.0
SUBLANES = 8
BOUNDED_KEY_TILE = 16384
BOUNDED_SCORE_TILE = 8 * 1024 * 1024


def _params(*sem):
    return pltpu.CompilerParams(dimension_semantics=sem, vmem_limit_bytes=VMEM_LIMIT_BYTES)


def _const_spec(shape):
    nd = len(shape)
    return pl.BlockSpec(shape, lambda *_: (0,) * nd, pipeline_mode=pl.Buffered(1))


def _rms(x, g):
    ms = jnp.mean(x * x, axis=-1, keepdims=True)
    return x * lax.rsqrt(ms + EPS) * g


def _rope_tables(seq):
    rows = seq // GRID_W
    inv_freq = 1.0 / (ROPE_THETA ** (jnp.arange(0, ROPE_AXIS_DIM, 2, dtype=F32) / ROPE_AXIS_DIM))
    ang_r = jnp.arange(rows, dtype=F32)[:, None] * inv_freq[None, :]
    ang_c = jnp.arange(GRID_W, dtype=F32)[:, None] * inv_freq[None, :]
    zr = jnp.zeros((rows, ROPE_AXIS_DIM), F32)
    zc = jnp.zeros((GRID_W, ROPE_AXIS_DIM), F32)
    cr, sr, cc, sc = jnp.cos(ang_r), jnp.sin(ang_r), jnp.cos(ang_c), jnp.sin(ang_c)
    cos_r = jnp.concatenate([cr, cr, zr], axis=-1)
    sin_r = jnp.concatenate([-sr, sr, zr], axis=-1)
    cos_c = jnp.concatenate([zc, cc, cc], axis=-1)
    sin_c = jnp.concatenate([zc, -sc, sc], axis=-1)
    return cos_r, sin_r, cos_c, sin_c


def _channel_dft_table():
    c = np.arange(F_CH)
    ang = 2.0 * np.pi * np.outer(c, c) / F_CH
    cs, sn = np.cos(ang) / math.sqrt(F_CH), np.sin(ang) / math.sqrt(F_CH)
    t = np.zeros((F_WIDTH, 2 * F_WIDTH), np.float64)
    for g in range(F_GROUPS):
        sl = slice(g * F_CH, (g + 1) * F_CH)
        t[sl, sl] = cs
        t[sl, F_WIDTH + g * F_CH:F_WIDTH + (g + 1) * F_CH] = -sn
    return jnp.asarray(t, dtype=F32).astype(BF16)


def _stage1_table(n1):
    k = np.arange(n1)
    ang = 2.0 * np.pi * np.outer(k, k) / n1
    cs, sn = np.cos(ang) / math.sqrt(n1), np.sin(ang) / math.sqrt(n1)
    return jnp.asarray(np.block([[cs, sn], [-sn, cs]]), dtype=F32).astype(BF16)


def _stage2_tables(n1):
    n2 = FFT_N2
    n = n1 * n2
    alpha = 2.0 * np.pi * np.outer(np.arange(n1), np.arange(n2)) / n
    beta = 2.0 * np.pi * np.outer(np.arange(n2), np.arange(n2)) / n2
    s = 1.0 / math.sqrt(n2)
    as_f32 = lambda a: jnp.asarray(a, dtype=F32)
    return as_f32(np.cos(alpha)), as_f32(np.sin(alpha)), as_f32(np.cos(beta) * s), as_f32(np.sin(beta) * s)


def _in_proj_kernel(x_ref, gmix_ref, win_ref, dftc_ref, gq_ref, gk_ref, cosr_ref, sinr_ref, cosc_ref, sinc_ref,
                    ur_ref, ui_ref, q_ref, k_ref, v_ref, *, q_scale, row_splits):
    tm = x_ref.shape[0]
    sub = tm // row_splits
    lane = lax.broadcasted_iota(jnp.int32, (sub, HEAD_DIM), 1)
    first_half = (lane % ROPE_AXIS_DIM) < (ROPE_AXIS_DIM // 2)
    gq = gq_ref[...] * q_scale
    gk = gk_ref[...]

    for r in range(row_splits):
        rows = slice(r * sub, (r + 1) * sub)
        h = _rms(x_ref[rows, :], gmix_ref[...]).astype(BF16)
        z = jnp.dot(h, win_ref[...], preferred_element_type=F32)

        u = jnp.dot(z[:, :F_WIDTH].astype(BF16), dftc_ref[...], preferred_element_type=F32)
        ur_ref[rows, :] = u[:, :F_WIDTH].astype(BF16)
        ui_ref[rows, :] = u[:, F_WIDTH:].astype(BF16)

        grid_rows = range(r * sub // GRID_W, (r + 1) * sub // GRID_W)
        cos = jnp.concatenate([cosr_ref[gr:gr + 1, :] + cosc_ref[...] for gr in grid_rows], axis=0)
        sin = jnp.concatenate([sinr_ref[gr:gr + 1, :] + sinc_ref[...] for gr in grid_rows], axis=0)

        def norm_rope(zh, g):
            y = _rms(zh, g)
            partner = jnp.where(first_half,
                                pltpu.roll(y, HEAD_DIM - ROPE_AXIS_DIM // 2, 1),
                                pltpu.roll(y, ROPE_AXIS_DIM // 2, 1))
            return y * cos + partner * sin

        for hh in range(N_Q_HEADS):
            lo = F_WIDTH + hh * HEAD_DIM
            q = norm_rope(z[:, lo:lo + HEAD_DIM], gq)
            q_ref[rows, hh * HEAD_DIM:(hh + 1) * HEAD_DIM] = q.astype(BF16)
        for hh in range(N_KV_HEADS):
            lo = F_WIDTH + Q_WIDTH + hh * HEAD_DIM
            k_ref[rows, hh * HEAD_DIM:(hh + 1) * HEAD_DIM] = norm_rope(z[:, lo:lo + HEAD_DIM], gk).astype(BF16)
            lo = F_WIDTH + Q_WIDTH + KV_WIDTH + hh * HEAD_DIM
            v_ref[rows, hh * HEAD_DIM:(hh + 1) * HEAD_DIM] = z[:, lo:lo + HEAD_DIM].astype(BF16)


def _in_proj(x2d, batch, seq, g_mix, w_in, dftc, g_q, g_k, rope, tm=512):
    t, d = x2d.shape
    nb = seq // tm
    in_w = w_in.shape[1]
    row = lambda i: (i, 0)
    pos = lambda i: (i % nb, 0)
    q_scale = LOG2E / math.sqrt(HEAD_DIM)
    return pl.pallas_call(
        functools.partial(_in_proj_kernel, q_scale=q_scale, row_splits=2),
        grid=(t // tm,),
        in_specs=[
            pl.BlockSpec((tm, d), row),
            _const_spec((1, d)),
            _const_spec((d, in_w)),
            _const_spec((F_WIDTH, 2 * F_WIDTH)),
            _const_spec((1, HEAD_DIM)),
            _const_spec((1, HEAD_DIM)),
            pl.BlockSpec((tm // GRID_W, HEAD_DIM), pos),
            pl.BlockSpec((tm // GRID_W, HEAD_DIM), pos),
            _const_spec((GRID_W, HEAD_DIM)),
            _const_spec((GRID_W, HEAD_DIM)),
        ],
        out_specs=[
            pl.BlockSpec((tm, F_WIDTH), row),
            pl.BlockSpec((tm, F_WIDTH), row),
            pl.BlockSpec((tm, Q_WIDTH), row),
            pl.BlockSpec((tm, KV_WIDTH), row),
            pl.BlockSpec((tm, KV_WIDTH), row),
        ],
        out_shape=[
            jax.ShapeDtypeStruct((t, F_WIDTH), BF16),
            jax.ShapeDtypeStruct((t, F_WIDTH), BF16),
            jax.ShapeDtypeStruct((t, Q_WIDTH), BF16),
            jax.ShapeDtypeStruct((t, KV_WIDTH), BF16),
            jax.ShapeDtypeStruct((t, KV_WIDTH), BF16),
        ],
        compiler_params=_params("parallel"),
        name="in_proj",
    )(x2d, g_mix, w_in, dftc, g_q, g_k, *rope)


def _fourier1_kernel(m1_ref, ur_ref, ui_ref, yr_ref, yi_ref):
    n1 = ur_ref.shape[0]
    u = jnp.concatenate([ur_ref[...], ui_ref[...]], axis=0)
    y = jnp.dot(m1_ref[...], u, preferred_element_type=F32)
    yr_ref[...] = y[:n1].astype(BF16)
    yi_ref[...] = y[n1:].astype(BF16)


def _fourier2_kernel(ca_ref, sa_ref, cb_ref, sb_ref, wbd_ref, yr_ref, yi_ref, o_ref, *, kb):
    n2 = FFT_N2
    g = pl.program_id(1)
    cb = cb_ref[...]
    sb = sb_ref[...]
    wbd = wbd_ref[...]
    for j in range(kb):
        k1 = g * kb + j
        ca = ca_ref[pl.ds(k1, 1), :]
        sa = sa_ref[pl.ds(k1, 1), :]
        tw = jnp.concatenate([cb * ca - sb * sa, sb * ca + cb * sa], axis=1).astype(BF16)
        yy = jnp.concatenate([yr_ref[j * n2:(j + 1) * n2, :], yi_ref[j * n2:(j + 1) * n2, :]], axis=0)
        fr = jnp.dot(tw, yy, preferred_element_type=F32)
        o = jnp.dot(fr.astype(BF16), wbd, preferred_element_type=F32)
        o_ref[:, j * F_WIDTH:(j + 1) * F_WIDTH] = o.astype(BF16)


def _fourier(ur, ui, batch, seq, wbd, kb=8):
    n2 = FFT_N2
    n1 = seq // n2
    cols = n2 * F_WIDTH
    tc = min(cols, (4 * 1024 * 1024) // (2 * n1 * 4))
    m1 = _stage1_table(n1)
    ca, sa, cb, sb = _stage2_tables(n1)
    urv = ur.reshape(batch, n1, cols)
    uiv = ui.reshape(batch, n1, cols)
    blk1 = pl.BlockSpec((None, n1, tc), lambda b, c: (b, 0, c))
    yr, yi = pl.pallas_call(
        _fourier1_kernel,
        grid=(batch, cols // tc),
        in_specs=[_const_spec((2 * n1, 2 * n1)), blk1, blk1],
        out_specs=[blk1, blk1],
        out_shape=[jax.ShapeDtypeStruct((batch, n1, cols), BF16)] * 2,
        compiler_params=_params("parallel", "parallel"),
        name="fourier_stage1",
    )(m1, urv, uiv)

    kb = min(kb, n1)
    yblk = pl.BlockSpec((None, kb * n2, F_WIDTH), lambda b, g: (b, g, 0))
    out = pl.pallas_call(
        functools.partial(_fourier2_kernel, kb=kb),
        grid=(batch, n1 // kb),
        in_specs=[_const_spec((n1, n2)), _const_spec((n1, n2)), _const_spec((n2, n2)), _const_spec((n2, n2)),
                  _const_spec((F_WIDTH, F_WIDTH)), yblk, yblk],
        out_specs=pl.BlockSpec((None, n2, kb * F_WIDTH), lambda b, g: (b, 0, g)),
        out_shape=jax.ShapeDtypeStruct((batch, n2, n1 * F_WIDTH), BF16),
        compiler_params=_params("parallel", "parallel"),
        name="fourier_stage2",
    )(ca, sa, cb, sb, wbd, yr.reshape(batch, seq, F_WIDTH), yi.reshape(batch, seq, F_WIDTH))
    return out.reshape(batch * seq, F_WIDTH)


def _scores(k, q_ref, g):
    q = q_ref[:, g * HEAD_DIM:(g + 1) * HEAD_DIM]
    return lax.dot_general(k, q, (((1,), (1,)), ((), ())), preferred_element_type=F32)


def _attention_kernel(q_ref, k_ref, v_ref, o_ref, m_ref, l_ref, acc_ref):
    j = pl.program_id(3)

    @pl.when(j == 0)
    def _():
        m_ref[...] = jnp.full(m_ref.shape, -jnp.inf, F32)
        l_ref[...] = jnp.zeros(l_ref.shape, F32)
        acc_ref[...] = jnp.zeros(acc_ref.shape, F32)

    k = k_ref[...]
    vt = v_ref[...].T
    for g in range(Q_PER_KV):
        s = _scores(k, q_ref, g)
        m_prev = m_ref[g]
        m_new = jnp.maximum(m_prev, jnp.max(s, axis=0, keepdims=True))
        alpha = jnp.exp2(m_prev - m_new)
        p = jnp.exp2(s - m_new)
        l_ref[g] = alpha * l_ref[g] + jnp.sum(p, axis=0, keepdims=True)
        acc_ref[g] = alpha * acc_ref[g] + jnp.dot(vt, p.astype(BF16), preferred_element_type=F32)
        m_ref[g] = m_new

    @pl.when(j == pl.num_programs(3) - 1)
    def _():
        for g in range(Q_PER_KV):
            o = acc_ref[g] * (1.0 / l_ref[g])
            o_ref[:, g * HEAD_DIM:(g + 1) * HEAD_DIM] = o.T.astype(BF16)


def _attention_bounded_kernel(q_ref, k_ref, v_ref, o_ref, l_ref, acc_ref):
    j = pl.program_id(3)

    @pl.when(j == 0)
    def _():
        l_ref[...] = jnp.zeros(l_ref.shape, F32)
        acc_ref[...] = jnp.zeros(acc_ref.shape, F32)

    k = k_ref[...]
    vt = v_ref[...].T
    tk = k.shape[0]
    for g in range(Q_PER_KV):
        p = jnp.exp2(_scores(k, q_ref, g))
        l_ref[g] += jnp.sum(p.reshape(tk // SUBLANES, SUBLANES, p.shape[1]), axis=0)
        acc_ref[g] += jnp.dot(vt, p.astype(BF16), preferred_element_type=F32)

    @pl.when(j == pl.num_programs(3) - 1)
    def _():
        for g in range(Q_PER_KV):
            denom = jnp.sum(l_ref[g], axis=0, keepdims=True)
            o = acc_ref[g] * (1.0 / denom)
            o_ref[:, g * HEAD_DIM:(g + 1) * HEAD_DIM] = o.T.astype(BF16)


def _attention_call(body, scratch, q, k, v, tq, tk, name):
    batch, seq, _ = q.shape
    tq, tk = min(tq, seq), min(tk, seq)
    gw = Q_PER_KV * HEAD_DIM
    return pl.pallas_call(
        body,
        grid=(batch, N_KV_HEADS, seq // tq, seq // tk),
        in_specs=[
            pl.BlockSpec((None, tq, gw), lambda b, h, i, j: (b, i, h)),
            pl.BlockSpec((None, tk, HEAD_DIM), lambda b, h, i, j: (b, j, h)),
            pl.BlockSpec((None, tk, HEAD_DIM), lambda b, h, i, j: (b, j, h)),
        ],
        out_specs=pl.BlockSpec((None, tq, gw), lambda b, h, i, j: (b, i, h)),
        out_shape=jax.ShapeDtypeStruct((batch, seq, Q_WIDTH), BF16),
        scratch_shapes=scratch(tq),
        compiler_params=_params("parallel", "parallel", "parallel", "arbitrary"),
        name=name,
    )(q, k, v)


def _attention(q, k, v, score_bound):
    online_scratch = lambda tq: [pltpu.VMEM((Q_PER_KV, 1, tq), F32), pltpu.VMEM((Q_PER_KV, 1, tq), F32),
                                 pltpu.VMEM((Q_PER_KV, HEAD_DIM, tq), F32)]
    bounded_scratch = lambda tq: [pltpu.VMEM((Q_PER_KV, SUBLANES, tq), F32),
                                  pltpu.VMEM((Q_PER_KV, HEAD_DIM, tq), F32)]
    seq = q.shape[1]
    tk = min(BOUNDED_KEY_TILE, seq)
    tq = min(seq, max(BOUNDED_SCORE_TILE // tk, 2 * HEAD_DIM))
    bounded = functools.partial(_attention_call, _attention_bounded_kernel, bounded_scratch,
                                tq=tq, tk=tk, name="self_attention_bounded")
    online = functools.partial(_attention_call, _attention_kernel, online_scratch,
                               tq=512, tk=512, name="self_attention_online")
    return lax.cond(score_bound <= SAFE_SCORE_BOUND, bounded, online, q, k, v)


def _mem_kv_kernel(mem_ref, gmem_ref, wckv_ref, gck_ref, kt_ref, v_ref):
    d = mem_ref.shape[1]
    xh = d // X_HEADS
    h = _rms(mem_ref[...], gmem_ref[...]).astype(BF16)
    kv = jnp.dot(h, wckv_ref[...], preferred_element_type=F32)
    gck = gck_ref[...]
    for hh in range(X_HEADS):
        kh = _rms(kv[:, hh * xh:(hh + 1) * xh], gck)
        kt_ref[hh * xh:(hh + 1) * xh, :] = kh.T.astype(BF16)
    v_ref[...] = kv[:, d:].astype(BF16)


def _mem_kv(mem, g_mem, w_ckv, g_ck):
    batch, m, d = mem.shape
    return pl.pallas_call(
        _mem_kv_kernel,
        grid=(batch,),
        in_specs=[pl.BlockSpec((None, m, d), lambda b: (b, 0, 0)),
                  _const_spec((1, d)), _const_spec((d, 2 * d)), _const_spec((1, d // X_HEADS))],
        out_specs=[pl.BlockSpec((None, d, m), lambda b: (b, 0, 0)),
                   pl.BlockSpec((None, m, d), lambda b: (b, 0, 0))],
        out_shape=[jax.ShapeDtypeStruct((batch, d, m), BF16), jax.ShapeDtypeStruct((batch, m, d), BF16)],
        compiler_params=_params("parallel"),
        name="mem_kv",
    )(mem, g_mem, w_ckv, g_ck)


def _post_kernel(x_ref, f_ref, a_ref, wout_ref, gcross_ref, wcq_ref, gcq_ref, kt_ref, v_ref, wco_ref,
                 gmlp_ref, wup_ref, wdown_ref, y_ref, *, x_scale, ff_chunk):
    d = x_ref.shape[1]
    xh = d // X_HEADS
    x1 = (x_ref[...]
          + jnp.dot(f_ref[...], wout_ref[:F_WIDTH, :], preferred_element_type=F32)
          + jnp.dot(a_ref[...], wout_ref[F_WIDTH:, :], preferred_element_type=F32))

    qc = jnp.dot(_rms(x1, gcross_ref[...]).astype(BF16), wcq_ref[...], preferred_element_type=F32)
    gcq = gcq_ref[...]
    heads = []
    for hh in range(X_HEADS):
        sl = slice(hh * xh, (hh + 1) * xh)
        qh = (_rms(qc[:, sl], gcq) * x_scale).astype(BF16)
        s = jnp.dot(qh, kt_ref[sl, :], preferred_element_type=F32)
        p = jnp.exp2(s - jnp.max(s, axis=-1, keepdims=True))
        o = jnp.dot(p.astype(BF16), v_ref[:, sl], preferred_element_type=F32)
        heads.append((o * (1.0 / jnp.sum(p, axis=-1, keepdims=True))).astype(BF16))
    x2 = x1 + jnp.dot(jnp.concatenate(heads, axis=-1), wco_ref[...], preferred_element_type=F32)

    h3 = _rms(x2, gmlp_ref[...]).astype(BF16)
    acc = x2
    for c in range(wup_ref.shape[1] // ff_chunk):
        sl = slice(c * ff_chunk, (c + 1) * ff_chunk)
        u = jnp.maximum(jnp.dot(h3, wup_ref[:, sl], preferred_element_type=F32), 0.0)
        acc = acc + jnp.dot((u * u).astype(BF16), wdown_ref[sl, :], preferred_element_type=F32)
    y_ref[...] = acc


def _post(x2d, f_out, a_out, kt, v, seq, w_out, g_cross, w_cq, g_cq, w_co, g_mlp, w_up, w_down, tm=512):
    t, d = x2d.shape
    nb = seq // tm
    m = v.shape[1]
    d_ff = w_up.shape[1]
    row = lambda i: (i, 0)
    x_scale = LOG2E / math.sqrt(d // X_HEADS)
    return pl.pallas_call(
        functools.partial(_post_kernel, x_scale=x_scale, ff_chunk=1024),
        grid=(t // tm,),
        in_specs=[
            pl.BlockSpec((tm, d), row),
            pl.BlockSpec((tm, F_WIDTH), row),
            pl.BlockSpec((tm, Q_WIDTH), row),
            _const_spec((d, d)),
            _const_spec((1, d)),
            _const_spec((d, d)),
            _const_spec((1, d // X_HEADS)),
            pl.BlockSpec((None, d, m), lambda i: (i // nb, 0, 0)),
            pl.BlockSpec((None, m, d), lambda i: (i // nb, 0, 0)),
            _const_spec((d, d)),
            _const_spec((1, d)),
            _const_spec((d, d_ff)),
            _const_spec((d_ff, d)),
        ],
        out_specs=pl.BlockSpec((tm, d), row),
        out_shape=jax.ShapeDtypeStruct((t, d), F32),
        compiler_params=_params("parallel"),
        name="post",
    )(x2d, f_out, a_out, w_out, g_cross, w_cq, g_cq, kt, v, w_co, g_mlp, w_up, w_down)


def _block_diag_weights(w_f):
    out = jnp.zeros((F_WIDTH, F_WIDTH), w_f.dtype)
    for g in range(F_GROUPS):
        out = out.at[g * F_CH:(g + 1) * F_CH, g * F_CH:(g + 1) * F_CH].set(w_f[g])
    return out


def _layer(x, mem, rope, dftc, p):
    batch, seq, d = x.shape
    x2d = x.reshape(batch * seq, d)
    ur, ui, q, k, v = _in_proj(x2d, batch, seq, p["g_mix"], p["w_in"], dftc, p["g_q"], p["g_k"],
                                rope)
    f_out = _fourier(ur, ui, batch, seq, p["wbd"])
    score_bound = (HEAD_DIM * LOG2E / math.sqrt(HEAD_DIM)) * jnp.max(jnp.abs(p["g_q"])) * jnp.max(jnp.abs(p["g_k"]))
    a_out = _attention(q.reshape(batch, seq, Q_WIDTH), k.reshape(batch, seq, KV_WIDTH),
                       v.reshape(batch, seq, KV_WIDTH),
                       score_bound).reshape(batch * seq, Q_WIDTH)
    kt, v = _mem_kv(mem, p["g_mem"], p["w_ckv"], p["g_ck"])
    y = _post(x2d, f_out, a_out, kt, v, seq, p["w_out"], p["g_cross"], p["w_cq"], p["g_cq"], p["w_co"],
              p["g_mlp"], p["w_up"], p["w_down"])
    return y.reshape(batch, seq, d)


def kernel(x_prompt, x_sample, mem_prompt, mem_sample, g_mix, w_in, w_fourier, g_q, g_k, w_out, g_cross, g_mem,
           w_cq, w_ckv, g_cq, g_ck, w_co, g_mlp, w_up, w_down):
    depth = w_in.shape[0]
    max_seq = max(x_prompt.shape[1], x_sample.shape[1])
    rope = _rope_tables(max_seq)
    dftc = _channel_dft_table()
    yp, ys = x_prompt, x_sample
    for l in range(depth):
        row = lambda g: g[l][None, :]
        p = dict(
            g_mix=row(g_mix), w_in=w_in[l].astype(BF16), wbd=_block_diag_weights(w_fourier[l]).astype(BF16),
            g_q=row(g_q), g_k=row(g_k), w_out=w_out[l].astype(BF16), g_cross=row(g_cross), g_mem=row(g_mem),
            w_cq=w_cq[l].astype(BF16), w_ckv=w_ckv[l].astype(BF16), g_cq=row(g_cq), g_ck=row(g_ck),
            w_co=w_co[l].astype(BF16), g_mlp=row(g_mlp), w_up=w_up[l].astype(BF16), w_down=w_down[l].astype(BF16),
        )
        yp = _layer(yp, mem_prompt, rope, dftc, p)
        ys = _layer(ys, mem_sample, rope, dftc, p)
    return (yp, ys)
```
